```python
import jax, jax.numpy as jnp
from jax import lax
import numpy as np

D_MODEL = 4096
BATCH = 1
SEQ = 8192
DEPTH = 1

HGRN_HEADS = 16
HGRN_KEY_DIM = 128
HGRN_VAL_DIM = 128
HGRN_KEY_WIDTH = HGRN_HEADS * HGRN_KEY_DIM
HGRN_VAL_WIDTH = HGRN_HEADS * HGRN_VAL_DIM
CHUNK = 64
CONV_WIDTH = 2048
CONV_GROUPS = 16
CONV_K = 3
SPLIT_SIZES = (
    HGRN_KEY_WIDTH,
    HGRN_KEY_WIDTH,
    HGRN_VAL_WIDTH,
    HGRN_VAL_WIDTH,
    CONV_WIDTH,
    CONV_WIDTH,
    CONV_WIDTH,
    CONV_WIDTH,
    D_MODEL,
    D_MODEL,
)
IN_COLS = sum(SPLIT_SIZES)
EPS = 1e-6

kernel_name = "hgrn2_shortconv_gated_hybrid"


def rmsnorm(x, w):
    xf = x.astype(jnp.float32)
    y = xf * lax.rsqrt(jnp.mean(xf * xf, axis=-1, keepdims=True) + EPS)
    return (y * w.astype(jnp.float32)).astype(x.dtype)


def hgrn2_chunkwise(q, log_f, k, v):
    bsz, t_len, n_heads, dk = q.shape
    dv = v.shape[-1]
    n_chunks = t_len // CHUNK

    def to_chunks(a):
        return a.reshape(bsz, n_chunks, CHUNK, n_heads, a.shape[-1]).transpose(1, 0, 3, 2, 4)

    causal = jnp.tril(jnp.ones((CHUNK, CHUNK), dtype=bool))[:, :, None]

    def step(state, inp):
        qc, gc, kc, vc = inp
        b = jnp.cumsum(gc, axis=2)
        o_inter = jnp.einsum('bhtd,bhde->bhte', qc * jnp.exp(b), state)
        diff = b[:, :, :, None, :] - b[:, :, None, :, :]
        decay = jnp.exp(jnp.where(causal, diff, -jnp.inf))
        scores = jnp.einsum('bhtd,bhsd,bhtsd->bhts', qc, kc, decay)
        o_intra = jnp.einsum('bhts,bhse->bhte', scores, vc)
        b_last = b[:, :, -1:, :]
        state = (jnp.exp(b_last[:, :, 0, :])[..., None] * state
                 + jnp.einsum('bhsd,bhse->bhde', kc * jnp.exp(b_last - b), vc))
        return state, o_inter + o_intra

    s0 = jnp.zeros((bsz, n_heads, dk, dv), jnp.float32)
    _, o = lax.scan(step, s0, (to_chunks(q), to_chunks(log_f), to_chunks(k), to_chunks(v)))
    return o.transpose(1, 0, 3, 2, 4).reshape(bsz, t_len, n_heads, dv)


def causal_depthwise_conv(u, w):
    rhs = w.reshape(CONV_K, 1, u.shape[-1]).astype(u.dtype)
    return lax.conv_general_dilated(
        u, rhs, window_strides=(1,), padding=[(CONV_K - 1, 0)],
        dimension_numbers=('NWC', 'WIO', 'NWC'), feature_group_count=u.shape[-1])


def setup_inputs(seed: int = 0) -> dict:
    key = jax.random.key(seed)
    ks = jax.random.split(key, 12)
    f32 = jnp.float32
    x = jax.random.normal(ks[0], (BATCH, SEQ, D_MODEL), f32)
    norm_w = 1.0 + 0.02 * jax.random.normal(ks[1], (DEPTH, D_MODEL), f32)
    w_in = jax.random.normal(ks[2], (DEPTH, D_MODEL, IN_COLS), f32) * D_MODEL ** -0.5
    lb_logits = 0.5 * jax.random.normal(ks[3], (DEPTH + 1, HGRN_KEY_WIDTH), f32)
    hgrn_norm_w = 1.0 + 0.02 * jax.random.normal(ks[4], (DEPTH, HGRN_VAL_DIM), f32)
    conv_w = jax.random.normal(ks[5], (DEPTH, CONV_K, CONV_WIDTH), f32) * CONV_K ** -0.5
    w_branch_a = jax.random.normal(ks[6], (DEPTH, HGRN_VAL_WIDTH, D_MODEL), f32) * HGRN_VAL_WIDTH ** -0.5
    w_branch_b = jax.random.normal(ks[7], (DEPTH, CONV_WIDTH, D_MODEL), f32) * CONV_WIDTH ** -0.5
    gate_bias = 0.01 * jax.random.normal(ks[8], (DEPTH, 2, D_MODEL), f32)
    w_out = jax.random.normal(ks[9], (DEPTH, D_MODEL, D_MODEL), f32) * D_MODEL ** -0.5
    final_norm_w = 1.0 + 0.02 * jax.random.normal(ks[10], (D_MODEL,), f32)
    return {"x": x, "norm_w": norm_w, "w_in": w_in, "lb_logits": lb_logits,
            "hgrn_norm_w": hgrn_norm_w, "conv_w": conv_w, "w_branch_a": w_branch_a,
            "w_branch_b": w_branch_b, "gate_bias": gate_bias, "w_out": w_out,
            "final_norm_w": final_norm_w}


def reference(x, norm_w, w_in, lb_logits, hgrn_norm_w, conv_w, w_branch_a,
              w_branch_b, gate_bias, w_out, final_norm_w):
    bsz, t_len, _ = x.shape
    f32 = jnp.float32
    lower_bounds = jnp.cumsum(jax.nn.softmax(lb_logits.astype(f32), axis=0), axis=0)
    split_points = list(np.cumsum(SPLIT_SIZES)[:-1])
    h = x
    for layer in range(DEPTH):
        u = rmsnorm(h, norm_w[layer])
        proj = jnp.einsum('btd,dc->btc', u, w_in[layer])
        (a_q, a_f, a_i, a_g, c_b, c_c, c_h, c_g, g_a, g_b) = jnp.split(proj, split_points, axis=-1)

        lb = lower_bounds[layer]
        f = lb + (1.0 - lb) * jax.nn.sigmoid(a_f.astype(f32))
        log_f = jnp.log(f)
        k = 1.0 - f
        hs = (bsz, t_len, HGRN_HEADS)
        o_a = hgrn2_chunkwise(a_q.astype(f32).reshape(hs + (HGRN_KEY_DIM,)),
                              log_f.reshape(hs + (HGRN_KEY_DIM,)),
                              k.reshape(hs + (HGRN_KEY_DIM,)),
                              a_i.astype(f32).reshape(hs + (HGRN_VAL_DIM,)))
        o_a = rmsnorm(o_a, hgrn_norm_w[layer]).reshape(bsz, t_len, HGRN_VAL_WIDTH).astype(x.dtype)
        y_a = o_a * jax.nn.silu(a_g)

        conv = causal_depthwise_conv(c_c * c_h, conv_w[layer])
        y_b = (c_b * conv) * jax.nn.silu(c_g)

        z_a = jnp.einsum('btc,cd->btd', y_a, w_branch_a[layer])
        z_b = jnp.einsum('btc,cd->btd', y_b, w_branch_b[layer])
        merged = (jax.nn.sigmoid(g_a + gate_bias[layer, 0]) * z_a
                  + jax.nn.sigmoid(g_b + gate_bias[layer, 1]) * z_b)
        h = h + jnp.einsum('btd,de->bte', merged, w_out[layer])
    return rmsnorm(h, final_norm_w)
```

```python
import functools

import jax
import jax.numpy as jnp
from jax import lax
from jax.experimental import pallas as pl
from jax.experimental.pallas import tpu as pltpu

EPS = 1e-6
LANES = 128
SUBLANES = 8
CHUNK = 128
DIAG_BLOCK = 16
VMEM_LIMIT_BYTES = 56 * 1024 * 1024

_f32 = jnp.float32
_bf16 = jnp.bfloat16


def _sigmoid(v):
    return 1.0 / (1.0 + jnp.exp(-v))


def _dot(a, b):
    return jnp.dot(a, b, preferred_element_type=_f32)


def _dot_nt(a, b):
    return lax.dot_general(a, b, (((1,), (1,)), ((), ())), preferred_element_type=_f32)


def _dot_tn(a, b):
    return lax.dot_general(a, b, (((0,), (0,)), ((), ())), preferred_element_type=_f32)


def _prenorm_kernel(x_ref, w_ref, o_ref):
    x = x_ref[...]
    ms = jnp.mean(x * x, axis=-1, keepdims=True)
    o_ref[...] = (x * lax.rsqrt(ms + EPS) * w_ref[...]).astype(o_ref.dtype)


def _prenorm(x2d, w_row, *, block_rows):
    t_len, d = x2d.shape
    return pl.pallas_call(
        _prenorm_kernel,
        grid=(t_len // block_rows,),
        in_specs=[pl.BlockSpec((block_rows, d), lambda i: (i, 0)),
                  pl.BlockSpec((1, d), lambda i: (0, 0))],
        out_specs=pl.BlockSpec((block_rows, d), lambda i: (i, 0)),
        out_shape=jax.ShapeDtypeStruct((t_len, d), _bf16),
        compiler_params=pltpu.CompilerParams(
            dimension_semantics=("parallel",), vmem_limit_bytes=VMEM_LIMIT_BYTES),
        name="prenorm",
    )(x2d, w_row)


def _in_proj_kernel(u_ref, w_ref, o_ref):
    o_ref[...] = _dot(u_ref[...], w_ref[...]).astype(o_ref.dtype)


def _in_proj(u, w, *, bm, bn):
    t_len, d = u.shape
    n_cols = w.shape[1]
    return pl.pallas_call(
        _in_proj_kernel,
        grid=(t_len // bm, n_cols // bn),
        in_specs=[pl.BlockSpec((bm, d), lambda i, j: (i, 0)),
                  pl.BlockSpec((d, bn), lambda i, j: (0, j))],
        out_specs=pl.BlockSpec((bm, bn), lambda i, j: (i, j)),
        out_shape=jax.ShapeDtypeStruct((t_len, n_cols), _bf16),
        compiler_params=pltpu.CompilerParams(
            dimension_semantics=("parallel", "parallel"), vmem_limit_bytes=VMEM_LIMIT_BYTES),
        name="in_proj",
    )(u, w)


def _row_block_reference(b, pair_rows, ref_row):
    c = b.shape[0]
    grouped = b.reshape(c // pair_rows, pair_rows, LANES)
    picked = grouped[:, ref_row:ref_row + 1, :]
    return jnp.broadcast_to(picked, grouped.shape).reshape(c, LANES)


def _hgrn_chunk(q, a_f, v, lb, tri, state_t):
    c = q.shape[0]
    f = lb + (1.0 - lb) * _sigmoid(a_f)
    g = jnp.log(f)
    k = 1.0 - f
    g_hi = g.astype(_bf16)
    g_lo = (g - g_hi.astype(_f32)).astype(_bf16)
    b2 = _dot(tri, jnp.concatenate([g_hi, g_lo], axis=1))
    b = b2[:, :LANES] + b2[:, LANES:]

    rows = lax.broadcasted_iota(jnp.int32, (c, c), 0)
    cols = lax.broadcasted_iota(jnp.int32, (c, c), 1)

    ref = _row_block_reference(b, DIAG_BLOCK, DIAG_BLOCK // 2 - 1)
    qd = (q * jnp.exp(b - ref)).astype(_bf16)
    kd = (k * jnp.exp(ref - b)).astype(_bf16)
    shift = DIAG_BLOCK.bit_length() - 1
    mask = ((rows >> shift) == (cols >> shift)) & (cols <= rows)
    scores = jnp.where(mask, _dot_nt(qd, kd), 0.0)

    m = DIAG_BLOCK
    while m < c:
        ref = _row_block_reference(b, 2 * m, m - 1)
        e = jnp.exp(-jnp.abs(b - ref))
        s_lvl = _dot_nt((q * e).astype(_bf16), (k * e).astype(_bf16))
        shift = m.bit_length() - 1
        rblk = rows >> shift
        mask = ((rblk & 1) == 1) & ((cols >> shift) == rblk - 1)
        scores = jnp.where(mask, s_lvl, scores)
        m *= 2

    b_last = b[c - 1:c, :]
    qe = (q * jnp.exp(b)).astype(_bf16)
    ke = (k * jnp.exp(b_last - b)).astype(_bf16)
    o = _dot(scores.astype(_bf16), v) + _dot_nt(qe, state_t.astype(_bf16))
    new_state_t = state_t * jnp.exp(b_last) + _dot_tn(v, ke)
    return o, new_state_t


def _mixers_kernel(q_ref, f_ref, i_ref, g_ref, cb_ref, cc_ref, ch_ref, cg_ref,
                   lbl_ref, hw_ref, cw_ref, ya_ref, yb_ref, state_ref, p_ref):
    tb = q_ref.shape[0]
    n = pl.program_id(1)

    @pl.when(n == 0)
    def _():
        state_ref[...] = jnp.zeros_like(state_ref)
        p_ref[0:SUBLANES, :] = jnp.zeros((SUBLANES, LANES), _f32)

    lbl = lbl_ref[...]
    ex = jnp.exp(lbl - jnp.max(lbl, axis=0, keepdims=True))
    lb = ex[0:1, :] / jnp.sum(ex, axis=0, keepdims=True)

    tri = (lax.broadcasted_iota(jnp.int32, (CHUNK, CHUNK), 1)
           <= lax.broadcasted_iota(jnp.int32, (CHUNK, CHUNK), 0)).astype(_bf16)
    hw = hw_ref[...]

    state_t = state_ref[...]
    for c0 in range(0, tb, CHUNK):
        rs = pl.ds(c0, CHUNK)
        o, state_t = _hgrn_chunk(q_ref[rs, :].astype(_f32), f_ref[rs, :].astype(_f32),
                                 i_ref[rs, :], lb, tri, state_t)
        ms = jnp.mean(o * o, axis=-1, keepdims=True)
        gate = g_ref[rs, :].astype(_f32)
        ya = (o * lax.rsqrt(ms + EPS) * hw) * (gate * _sigmoid(gate))
        ya_ref[rs, :] = ya.astype(ya_ref.dtype)
    state_ref[...] = state_t

    p_ref[SUBLANES:SUBLANES + tb, :] = cc_ref[...].astype(_f32) * ch_ref[...].astype(_f32)
    k_taps = cw_ref.shape[0]
    conv = None
    for j in range(k_taps):
        off = SUBLANES - (k_taps - 1) + j
        term = cw_ref[j:j + 1, :] * p_ref[off:off + tb, :]
        conv = term if conv is None else conv + term
    cg = cg_ref[...].astype(_f32)
    yb = (cb_ref[...].astype(_f32) * conv) * (cg * _sigmoid(cg))
    yb_ref[...] = yb.astype(yb_ref.dtype)
    p_ref[0:SUBLANES, :] = p_ref[tb:tb + SUBLANES, :]


def _mixers(proj, lb_logits, hgrn_w_row, conv_taps, *, n_heads, col_offsets, block_rows):
    t_len = proj.shape[0]
    n_layers_p1 = lb_logits.shape[0]
    k_taps = conv_taps.shape[0]
    width = n_heads * LANES

    def plane(off):
        base = off // LANES
        return pl.BlockSpec((block_rows, LANES), lambda h, n: (n, base + h))

    out_spec = pl.BlockSpec((block_rows, LANES), lambda h, n: (n, h))
    return pl.pallas_call(
        _mixers_kernel,
        grid=(n_heads, t_len // block_rows),
        in_specs=[plane(off) for off in col_offsets] + [
            pl.BlockSpec((n_layers_p1, LANES), lambda h, n: (0, h)),
            pl.BlockSpec((1, LANES), lambda h, n: (0, 0)),
            pl.BlockSpec((k_taps, LANES), lambda h, n: (0, h)),
        ],
        out_specs=[out_spec, out_spec],
        out_shape=[jax.ShapeDtypeStruct((t_len, width), _bf16)] * 2,
        scratch_shapes=[pltpu.VMEM((LANES, LANES), _f32),
                        pltpu.VMEM((block_rows + SUBLANES, LANES), _f32)],
        compiler_params=pltpu.CompilerParams(
            dimension_semantics=("parallel", "arbitrary"), vmem_limit_bytes=VMEM_LIMIT_BYTES),
        name="mixers",
    )(*([proj] * len(col_offsets)), lb_logits, hgrn_w_row, conv_taps)


def _merge_kernel(ya_ref, yb_ref, ga_ref, gb_ref, bias_ref, wa_ref, wb_ref, o_ref):
    za = _dot(ya_ref[...], wa_ref[...])
    zb = _dot(yb_ref[...], wb_ref[...])
    gate_a = _sigmoid(ga_ref[...].astype(_f32) + bias_ref[0:1, :])
    gate_b = _sigmoid(gb_ref[...].astype(_f32) + bias_ref[1:2, :])
    o_ref[...] = (gate_a * za + gate_b * zb).astype(o_ref.dtype)


def _merge(ya, yb, proj, gate_bias, wa, wb, *, ga_off, gb_off, bm, bn):
    t_len, width = ya.shape
    d = wa.shape[1]
    ga_base, gb_base = ga_off // bn, gb_off // bn
    return pl.pallas_call(
        _merge_kernel,
        grid=(t_len // bm, d // bn),
        in_specs=[
            pl.BlockSpec((bm, width), lambda i, n: (i, 0)),
            pl.BlockSpec((bm, width), lambda i, n: (i, 0)),
            pl.BlockSpec((bm, bn), lambda i, n: (i, ga_base + n)),
            pl.BlockSpec((bm, bn), lambda i, n: (i, gb_base + n)),
            pl.BlockSpec((2, bn), lambda i, n: (0, n)),
            pl.BlockSpec((width, bn), lambda i, n: (0, n)),
            pl.BlockSpec((width, bn), lambda i, n: (0, n)),
        ],
        out_specs=pl.BlockSpec((bm, bn), lambda i, n: (i, n)),
        out_shape=jax.ShapeDtypeStruct((t_len, d), _bf16),
        compiler_params=pltpu.CompilerParams(
            dimension_semantics=("parallel", "parallel"), vmem_limit_bytes=VMEM_LIMIT_BYTES),
        name="merge",
    )(ya, yb, proj, proj, gate_bias, wa, wb)


def _out_proj_kernel(m_ref, wo_ref, x_ref, fw_ref, o_ref):
    n = pl.program_id(1)
    bn = x_ref.shape[1]
    col = pl.multiple_of(n * bn, LANES)
    o_ref[:, pl.ds(col, bn)] = x_ref[...] + _dot(m_ref[...], wo_ref[...])

    @pl.when(n == pl.num_programs(1) - 1)
    def _():
        h = o_ref[...]
        ms = jnp.mean(h * h, axis=-1, keepdims=True)
        o_ref[...] = h * lax.rsqrt(ms + EPS) * fw_ref[...]


def _out_proj(merged, wo, x2d, fw_row, *, bm, bn):
    t_len, d = x2d.shape
    return pl.pallas_call(
        _out_proj_kernel,
        grid=(t_len // bm, d // bn),
        in_specs=[
            pl.BlockSpec((bm, d), lambda i, n: (i, 0)),
            pl.BlockSpec((d, bn), lambda i, n: (0, n)),
            pl.BlockSpec((bm, bn), lambda i, n: (i, n)),
            pl.BlockSpec((1, d), lambda i, n: (0, 0)),
        ],
        out_specs=pl.BlockSpec((bm, d), lambda i, n: (i, 0)),
        out_shape=jax.ShapeDtypeStruct((t_len, d), _f32),
        compiler_params=pltpu.CompilerParams(
            dimension_semantics=("parallel", "arbitrary"), vmem_limit_bytes=VMEM_LIMIT_BYTES),
        name="out_proj",
    )(merged, wo, x2d, fw_row)


def kernel(x, norm_w, w_in, lb_logits, hgrn_norm_w, conv_w, w_branch_a, w_branch_b,
           gate_bias, w_out, final_norm_w):
    bsz, t_len, d = x.shape
    assert bsz == 1 and norm_w.shape[0] == 1, "single sequence, single layer"
    key_w = lb_logits.shape[1]
    val_w = w_branch_a.shape[1]
    conv_wd = conv_w.shape[2]
    assert hgrn_norm_w.shape[1] == LANES and key_w == val_w == conv_wd
    n_heads = key_w // LANES
    sizes = (key_w, key_w, val_w, val_w, conv_wd, conv_wd, conv_wd, conv_wd, d, d)
    offs = [sum(sizes[:i]) for i in range(len(sizes))]
    assert w_in.shape[2] == sum(sizes)

    x2d = x.reshape(t_len, d)
    u = _prenorm(x2d, norm_w, block_rows=min(256, t_len))
    proj = _in_proj(u, w_in[0].astype(_bf16), bm=min(1024, t_len), bn=min(1024, d))
    ya, yb = _mixers(proj, lb_logits, hgrn_norm_w, conv_w[0], n_heads=n_heads,
                     col_offsets=offs[:8], block_rows=min(512, t_len))
    merged = _merge(ya, yb, proj, gate_bias[0], w_branch_a[0].astype(_bf16),
                    w_branch_b[0].astype(_bf16), ga_off=offs[8], gb_off=offs[9],
                    bm=min(512, t_len), bn=min(1024, d))
    out = _out_proj(merged, w_out[0].astype(_bf16), x2d, final_norm_w.reshape(1, d),
                    bm=min(512, t_len), bn=min(512, d))
    return out.reshape(bsz, t_len, d)
```

```python
import functools

import jax
import jax.numpy as jnp
from jax import lax
from jax.experimental import pallas as pl
from jax.experimental.pallas import tpu as pltpu

EPS = 1e-6
LANES = 128
SUBLANES = 8
CHUNK = 128
DIAG_BLOCK = 16
VMEM_LIMIT_BYTES = 56 * 1024 * 1024

_f32 = jnp.float32
_bf16 = jnp.bfloat16


def _sigmoid(v):
    return 1.0 / (1.0 + jnp.exp(-v))


def _dot(a, b):
    return jnp.dot(a, b, preferred_element_type=_f32)


def _dot_nt(a, b):
    return lax.dot_general(a, b, (((1,), (1,)), ((), ())), preferred_element_type=_f32)


def _dot_tn(a, b):
    return lax.dot_general(a, b, (((0,), (0,)), ((), ())), preferred_element_type=_f32)


def _prenorm_kernel(x_ref, w_ref, o_ref):
    x = x_ref[...]
    ms = jnp.mean(x * x, axis=-1, keepdims=True)
    o_ref[...] = (x * lax.rsqrt(ms + EPS) * w_ref[...]).astype(o_ref.dtype)


def _prenorm(x2d, w_row, *, block_rows):
    t_len, d = x2d.shape
    return pl.pallas_call(
        _prenorm_kernel,
        grid=(t_len // block_rows,),
        in_specs=[pl.BlockSpec((block_rows, d), lambda i: (i, 0)),
                  pl.BlockSpec((1, d), lambda i: (0, 0))],
        out_specs=pl.BlockSpec((block_rows, d), lambda i: (i, 0)),
        out_shape=jax.ShapeDtypeStruct((t_len, d), _bf16),
        compiler_params=pltpu.CompilerParams(
            dimension_semantics=("parallel",), vmem_limit_bytes=VMEM_LIMIT_BYTES),
        name="prenorm",
    )(x2d, w_row)


def _in_proj_kernel(u_ref, w_ref, o_ref, wb_ref):
    @pl.when(pl.program_id(1) == 0)
    def _():
        wb_ref[...] = w_ref[...].astype(wb_ref.dtype)

    o_ref[...] = _dot(u_ref[...], wb_ref[...]).astype(o_ref.dtype)


def _in_proj(u, w, *, bm, bn):
    t_len, d = u.shape
    n_cols = w.shape[1]
    return pl.pallas_call(
        _in_proj_kernel,
        grid=(n_cols // bn, t_len // bm),
        in_specs=[pl.BlockSpec((bm, d), lambda j, i: (i, 0)),
                  pl.BlockSpec((d, bn), lambda j, i: (0, j))],
        out_specs=pl.BlockSpec((bm, bn), lambda j, i: (i, j)),
        out_shape=jax.ShapeDtypeStruct((t_len, n_cols), _bf16),
        scratch_shapes=[pltpu.VMEM((d, bn), _bf16)],
        compiler_params=pltpu.CompilerParams(
            dimension_semantics=("parallel", "arbitrary"), vmem_limit_bytes=VMEM_LIMIT_BYTES),
        name="in_proj",
    )(u, w)


def _row_block_reference(b, pair_rows, ref_row):
    c = b.shape[0]
    grouped = b.reshape(c // pair_rows, pair_rows, LANES)
    picked = grouped[:, ref_row:ref_row + 1, :]
    return jnp.broadcast_to(picked, grouped.shape).reshape(c, LANES)


def _hgrn_chunk(q, a_f, v, lb, tri, state_t):
    c = q.shape[0]
    f = lb + (1.0 - lb) * _sigmoid(a_f)
    g = jnp.log(f)
    k = 1.0 - f
    g_hi = g.astype(_bf16)
    g_lo = (g - g_hi.astype(_f32)).astype(_bf16)
    b2 = _dot(tri, jnp.concatenate([g_hi, g_lo], axis=1))
    b = b2[:, :LANES] + b2[:, LANES:]

    rows = lax.broadcasted_iota(jnp.int32, (c, c), 0)
    cols = lax.broadcasted_iota(jnp.int32, (c, c), 1)

    ref = _row_block_reference(b, DIAG_BLOCK, DIAG_BLOCK // 2 - 1)
    qd = (q * jnp.exp(b - ref)).astype(_bf16)
    kd = (k * jnp.exp(ref - b)).astype(_bf16)
    shift = DIAG_BLOCK.bit_length() - 1
    mask = ((rows >> shift) == (cols >> shift)) & (cols <= rows)
    scores = jnp.where(mask, _dot_nt(qd, kd), 0.0)

    m = DIAG_BLOCK
    while m < c:
        ref = _row_block_reference(b, 2 * m, m - 1)
        e = jnp.exp(-jnp.abs(b - ref))
        s_lvl = _dot_nt((q * e).astype(_bf16), (k * e).astype(_bf16))
        shift = m.bit_length() - 1
        rblk = rows >> shift
        mask = ((rblk & 1) == 1) & ((cols >> shift) == rblk - 1)
        scores = jnp.where(mask, s_lvl, scores)
        m *= 2

    b_last = b[c - 1:c, :]
    qe = (q * jnp.exp(b)).astype(_bf16)
    ke = (k * jnp.exp(b_last - b)).astype(_bf16)
    o = _dot(scores.astype(_bf16), v) + _dot_nt(qe, state_t.astype(_bf16))
    new_state_t = state_t * jnp.exp(b_last) + _dot_tn(v, ke)
    return o, new_state_t


def _mixers_kernel(q_ref, f_ref, i_ref, g_ref, cb_ref, cc_ref, ch_ref, cg_ref,
                   lbl_ref, hw_ref, cw_ref, ya_ref, yb_ref, state_ref, p_ref):
    tb = q_ref.shape[0]
    n = pl.program_id(1)

    @pl.when(n == 0)
    def _():
        state_ref[...] = jnp.zeros_like(state_ref)
        p_ref[0:SUBLANES, :] = jnp.zeros((SUBLANES, LANES), _f32)

    lbl = lbl_ref[...]
    ex = jnp.exp(lbl - jnp.max(lbl, axis=0, keepdims=True))
    lb = ex[0:1, :] / jnp.sum(ex, axis=0, keepdims=True)

    tri = (lax.broadcasted_iota(jnp.int32, (CHUNK, CHUNK), 1)
           <= lax.broadcasted_iota(jnp.int32, (CHUNK, CHUNK), 0)).astype(_bf16)
    hw = hw_ref[...]

    state_t = state_ref[...]
    for c0 in range(0, tb, CHUNK):
        rs = pl.ds(c0, CHUNK)
        o, state_t = _hgrn_chunk(q_ref[rs, :].astype(_f32), f_ref[rs, :].astype(_f32),
                                 i_ref[rs, :], lb, tri, state_t)
        ms = jnp.mean(o * o, axis=-1, keepdims=True)
        gate = g_ref[rs, :].astype(_f32)
        ya = (o * lax.rsqrt(ms + EPS) * hw) * (gate * _sigmoid(gate))
        ya_ref[rs, :] = ya.astype(ya_ref.dtype)
    state_ref[...] = state_t

    p_ref[SUBLANES:SUBLANES + tb, :] = cc_ref[...].astype(_f32) * ch_ref[...].astype(_f32)
    k_taps = cw_ref.shape[0]
    conv = None
    for j in range(k_taps):
        off = SUBLANES - (k_taps - 1) + j
        term = cw_ref[j:j + 1, :] * p_ref[off:off + tb, :]
        conv = term if conv is None else conv + term
    cg = cg_ref[...].astype(_f32)
    yb = (cb_ref[...].astype(_f32) * conv) * (cg * _sigmoid(cg))
    yb_ref[...] = yb.astype(yb_ref.dtype)
    p_ref[0:SUBLANES, :] = p_ref[tb:tb + SUBLANES, :]


def _mixers(proj, lb_logits, hgrn_w_row, conv_taps, *, n_heads, col_offsets, block_rows):
    t_len = proj.shape[0]
    n_layers_p1 = lb_logits.shape[0]
    k_taps = conv_taps.shape[0]
    width = n_heads * LANES

    def plane(off):
        base = off // LANES
        return pl.BlockSpec((block_rows, LANES), lambda h, n: (n, base + h))

    out_spec = pl.BlockSpec((block_rows, LANES), lambda h, n: (n, h))
    return pl.pallas_call(
        _mixers_kernel,
        grid=(n_heads, t_len // block_rows),
        in_specs=[plane(off) for off in col_offsets] + [
            pl.BlockSpec((n_layers_p1, LANES), lambda h, n: (0, h)),
            pl.BlockSpec((1, LANES), lambda h, n: (0, 0)),
            pl.BlockSpec((k_taps, LANES), lambda h, n: (0, h)),
        ],
        out_specs=[out_spec, out_spec],
        out_shape=[jax.ShapeDtypeStruct((t_len, width), _bf16)] * 2,
        scratch_shapes=[pltpu.VMEM((LANES, LANES), _f32),
                        pltpu.VMEM((block_rows + SUBLANES, LANES), _f32)],
        compiler_params=pltpu.CompilerParams(
            dimension_semantics=("parallel", "arbitrary"), vmem_limit_bytes=VMEM_LIMIT_BYTES),
        name="mixers",
    )(*([proj] * len(col_offsets)), lb_logits, hgrn_w_row, conv_taps)


def _merge_kernel(ya_ref, yb_ref, ga_ref, gb_ref, bias_ref, wa_ref, wb_ref, wo_ref,
                  o_ref, wo_bf_ref, wa_bf_ref, wb_bf_ref):
    @pl.when(pl.program_id(1) == 0)
    def _():
        wa_bf_ref[...] = wa_ref[...].astype(wa_bf_ref.dtype)
        wb_bf_ref[...] = wb_ref[...].astype(wb_bf_ref.dtype)

    wo_bf_ref[...] = wo_ref[...].astype(wo_bf_ref.dtype)

    za = _dot(ya_ref[...], wa_bf_ref[...])
    zb = _dot(yb_ref[...], wb_bf_ref[...])
    gate_a = _sigmoid(ga_ref[...].astype(_f32) + bias_ref[0:1, :])
    gate_b = _sigmoid(gb_ref[...].astype(_f32) + bias_ref[1:2, :])
    o_ref[...] = (gate_a * za + gate_b * zb).astype(o_ref.dtype)


def _merge(ya, yb, proj, gate_bias, wa, wb, wo, *, ga_off, gb_off, bm, bn):
    t_len, width = ya.shape
    d = wa.shape[1]
    ga_base, gb_base = ga_off // bn, gb_off // bn
    n_i = t_len // bm
    n_steps = (d // bn) * n_i
    wo_rows = wo.shape[0] // n_steps
    assert wo_rows * n_steps == wo.shape[0] and wo_rows % 16 == 0
    return pl.pallas_call(
        _merge_kernel,
        grid=(d // bn, n_i),
        in_specs=[
            pl.BlockSpec((bm, width), lambda n, i: (i, 0)),
            pl.BlockSpec((bm, width), lambda n, i: (i, 0)),
            pl.BlockSpec((bm, bn), lambda n, i: (i, ga_base + n)),
            pl.BlockSpec((bm, bn), lambda n, i: (i, gb_base + n)),
            pl.BlockSpec((2, bn), lambda n, i: (0, n)),
            pl.BlockSpec((width, bn), lambda n, i: (0, n)),
            pl.BlockSpec((width, bn), lambda n, i: (0, n)),
            pl.BlockSpec((wo_rows, wo.shape[1]), lambda n, i: (n * n_i + i, 0)),
        ],
        out_specs=[pl.BlockSpec((bm, bn), lambda n, i: (i, n)),
                   pl.BlockSpec((wo_rows, wo.shape[1]), lambda n, i: (n * n_i + i, 0))],
        out_shape=[jax.ShapeDtypeStruct((t_len, d), _bf16),
                   jax.ShapeDtypeStruct(wo.shape, _bf16)],
        scratch_shapes=[pltpu.VMEM((width, bn), _bf16), pltpu.VMEM((width, bn), _bf16)],
        compiler_params=pltpu.CompilerParams(
            dimension_semantics=("parallel", "arbitrary"), vmem_limit_bytes=VMEM_LIMIT_BYTES),
        name="merge",
    )(ya, yb, proj, proj, gate_bias, wa, wb, wo)


def _out_proj_kernel(m_ref, wo_ref, x_ref, fw_ref, o_ref):
    n = pl.program_id(1)
    bn = x_ref.shape[1]
    col = pl.multiple_of(n * bn, LANES)
    o_ref[:, pl.ds(col, bn)] = x_ref[...] + _dot(m_ref[...], wo_ref[...])

    @pl.when(n == pl.num_programs(1) - 1)
    def _():
        h = o_ref[...]
        ms = jnp.mean(h * h, axis=-1, keepdims=True)
        o_ref[...] = h * lax.rsqrt(ms + EPS) * fw_ref[...]


def _out_proj(merged, wo, x2d, fw_row, *, bm, bn):
    t_len, d = x2d.shape
    return pl.pallas_call(
        _out_proj_kernel,
        grid=(t_len // bm, d // bn),
        in_specs=[
            pl.BlockSpec((bm, d), lambda i, n: (i, 0)),
            pl.BlockSpec((d, bn), lambda i, n: (0, n)),
            pl.BlockSpec((bm, bn), lambda i, n: (i, n)),
            pl.BlockSpec((1, d), lambda i, n: (0, 0)),
        ],
        out_specs=pl.BlockSpec((bm, d), lambda i, n: (i, 0)),
        out_shape=jax.ShapeDtypeStruct((t_len, d), _f32),
        compiler_params=pltpu.CompilerParams(
            dimension_semantics=("parallel", "arbitrary"), vmem_limit_bytes=VMEM_LIMIT_BYTES),
        name="out_proj",
    )(merged, wo, x2d, fw_row)


def kernel(x, norm_w, w_in, lb_logits, hgrn_norm_w, conv_w, w_branch_a, w_branch_b,
           gate_bias, w_out, final_norm_w):
    bsz, t_len, d = x.shape
    assert bsz == 1 and norm_w.shape[0] == 1, "single sequence, single layer"
    key_w = lb_logits.shape[1]
    val_w = w_branch_a.shape[1]
    conv_wd = conv_w.shape[2]
    assert hgrn_norm_w.shape[1] == LANES and key_w == val_w == conv_wd
    n_heads = key_w // LANES
    sizes = (key_w, key_w, val_w, val_w, conv_wd, conv_wd, conv_wd, conv_wd, d, d)
    offs = [sum(sizes[:i]) for i in range(len(sizes))]
    assert w_in.shape[2] == sum(sizes)

    x2d = x.reshape(t_len, d)
    u = _prenorm(x2d, norm_w, block_rows=min(256, t_len))
    proj = _in_proj(u, w_in[0], bm=min(1024, t_len), bn=min(512, d))
    ya, yb = _mixers(proj, lb_logits, hgrn_norm_w, conv_w[0], n_heads=n_heads,
                     col_offsets=offs[:8], block_rows=min(512, t_len))
    merged, wo_bf = _merge(ya, yb, proj, gate_bias[0], w_branch_a[0], w_branch_b[0], w_out[0],
                           ga_off=offs[8], gb_off=offs[9], bm=min(512, t_len), bn=min(512, d))
    out = _out_proj(merged, wo_bf, x2d, final_norm_w.reshape(1, d),
                    bm=min(512, t_len), bn=min(512, d))
    return out.reshape(bsz, t_len, d)
```

```python
import functools

import jax
import jax.numpy as jnp
from jax import lax
from jax.experimental import pallas as pl
from jax.experimental.pallas import tpu as pltpu

EPS = 1e-6
LANES = 128
SUBLANES = 8
CHUNK = 128
DIAG_BLOCK = 16
N_PLANES = 8
PIECE_ROWS = 128
PIECE_COLS = 512
VMEM_LIMIT_BYTES = 56 * 1024 * 1024

_f32 = jnp.float32
_bf16 = jnp.bfloat16


def _sigmoid(v):
    return 1.0 / (1.0 + jnp.exp(-v))


def _dot(a, b):
    return jnp.dot(a, b, preferred_element_type=_f32)


def _dot_nt(a, b):
    return lax.dot_general(a, b, (((1,), (1,)), ((), ())), preferred_element_type=_f32)


def _dot_tn(a, b):
    return lax.dot_general(a, b, (((0,), (0,)), ((), ())), preferred_element_type=_f32)


def _prenorm_kernel(x_ref, w_ref, o_ref):
    x = x_ref[...]
    ms = jnp.mean(x * x, axis=-1, keepdims=True)
    o_ref[...] = (x * lax.rsqrt(ms + EPS) * w_ref[...]).astype(o_ref.dtype)


def _prenorm(x2d, w_row, *, block_rows):
    t_len, d = x2d.shape
    return pl.pallas_call(
        _prenorm_kernel,
        grid=(t_len // block_rows,),
        in_specs=[pl.BlockSpec((block_rows, d), lambda i: (i, 0)),
                  pl.BlockSpec((1, d), lambda i: (0, 0))],
        out_specs=pl.BlockSpec((block_rows, d), lambda i: (i, 0)),
        out_shape=jax.ShapeDtypeStruct((t_len, d), _bf16),
        compiler_params=pltpu.CompilerParams(
            dimension_semantics=("parallel",), vmem_limit_bytes=VMEM_LIMIT_BYTES),
        name="prenorm",
    )(x2d, w_row)


def _row_block_reference(b, pair_rows, ref_row):
    c = b.shape[0]
    grouped = b.reshape(c // pair_rows, pair_rows, LANES)
    picked = grouped[:, ref_row:ref_row + 1, :]
    return jnp.broadcast_to(picked, grouped.shape).reshape(c, LANES)


def _chunk_gates(a_f, lb, tri):
    f = lb + (1.0 - lb) * _sigmoid(a_f)
    g = jnp.log(f)
    g_hi = g.astype(_bf16)
    g_lo = (g - g_hi.astype(_f32)).astype(_bf16)
    return 1.0 - f, _dot(tri, jnp.concatenate([g_hi, g_lo], axis=1))


def _chunk_scores(q, k, b2):
    c = q.shape[0]
    b = b2[:, :LANES] + b2[:, LANES:]
    ref = _row_block_reference(b, DIAG_BLOCK, DIAG_BLOCK // 2 - 1)
    raw = [_dot_nt((q * jnp.exp(b - ref)).astype(_bf16), (k * jnp.exp(ref - b)).astype(_bf16))]
    m = DIAG_BLOCK
    while m < c:
        ref = _row_block_reference(b, 2 * m, m - 1)
        e = jnp.exp(-jnp.abs(b - ref))
        raw.append(_dot_nt((q * e).astype(_bf16), (k * e).astype(_bf16)))
        m *= 2
    b_last = b[c - 1:c, :]
    qe = (q * jnp.exp(b)).astype(_bf16)
    ke = (k * jnp.exp(b_last - b)).astype(_bf16)
    return raw, qe, ke, jnp.exp(b_last)


def _chunk_output(raw, qe, ke, decay, v, state_t):
    c = qe.shape[0]
    rows = lax.broadcasted_iota(jnp.int32, (c, c), 0)
    cols = lax.broadcasted_iota(jnp.int32, (c, c), 1)
    shift = DIAG_BLOCK.bit_length() - 1
    mask = ((rows >> shift) == (cols >> shift)) & (cols <= rows)
    scores = jnp.where(mask, raw[0], 0.0)
    for lvl, s_lvl in enumerate(raw[1:]):
        rblk = rows >> (shift + lvl)
        mask = ((rblk & 1) == 1) & ((cols >> (shift + lvl)) == rblk - 1)
        scores = jnp.where(mask, s_lvl, scores)
    o = _dot(scores.astype(_bf16), v) + _dot_nt(qe, state_t.astype(_bf16))
    return o, state_t * decay + _dot_tn(v, ke)


def _mixer_stage_groups(t_ref, lbl_ref, hw_ref, cw_ref, ya_ref, yb_ref, state_ref, p_ref,
                        first_block):
    tb = t_ref.shape[0]
    n_chunks = tb // CHUNK

    def plane(idx, rs=slice(None)):
        return t_ref[rs, idx * LANES:(idx + 1) * LANES]

    lbl = lbl_ref[...]
    ex = jnp.exp(lbl - jnp.max(lbl, axis=0, keepdims=True))
    lb = ex[0:1, :] / jnp.sum(ex, axis=0, keepdims=True)
    tri = (lax.broadcasted_iota(jnp.int32, (CHUNK, CHUNK), 1)
           <= lax.broadcasted_iota(jnp.int32, (CHUNK, CHUNK), 0)).astype(_bf16)
    hw = hw_ref[...]

    live = {"state": jnp.where(first_block, 0.0, state_ref[...])}

    def rows_of(c):
        return pl.ds(c * CHUNK, CHUNK)

    def gates(c):
        live[c] = _chunk_gates(plane(1, rows_of(c)), lb, tri)

    def scores(c):
        k, b2 = live[c]
        live[c] = _chunk_scores(plane(0, rows_of(c)), k, b2)

    def output(c):
        rs = rows_of(c)
        o, live["state"] = _chunk_output(*live.pop(c), plane(2, rs).astype(_bf16), live["state"])
        ms = jnp.mean(o * o, axis=-1, keepdims=True)
        gate = plane(3, rs)
        ya = (o * lax.rsqrt(ms + EPS) * hw) * (gate * _sigmoid(gate))
        ya_ref[rs, :] = ya.astype(ya_ref.dtype)
        if c == n_chunks - 1:
            state_ref[...] = live["state"]

    groups = []
    for g in range(n_chunks + 2):
        stages = [(output, g - 2), (scores, g - 1), (gates, g)]
        groups.append([functools.partial(fn, c) for fn, c in stages if 0 <= c < n_chunks])

    def conv():
        p_ref[0:SUBLANES, :] = jnp.where(first_block, 0.0, p_ref[tb:tb + SUBLANES, :])
        p_ref[SUBLANES:SUBLANES + tb, :] = plane(5) * plane(6)
        k_taps = cw_ref.shape[0]
        acc = None
        for j in range(k_taps):
            off = SUBLANES - (k_taps - 1) + j
            term = cw_ref[j:j + 1, :] * p_ref[off:off + tb, :]
            acc = term if acc is None else acc + term
        cg = plane(7)
        yb = (plane(4) * acc) * (cg * _sigmoid(cg))
        yb_ref[...] = yb.astype(yb_ref.dtype)

    return groups, conv


def _head_weight_copies(w_hbm, stage_ref, sem, head, col_offsets):
    return [pltpu.make_async_copy(
        w_hbm.at[:, pl.ds(pl.multiple_of(off + head * LANES, LANES), LANES)],
        stage_ref.at[:, pl.ds(idx * LANES, LANES)], sem.at[idx])
        for idx, off in enumerate(col_offsets)]


def _proj_mixers_kernel(u_ref, w_hbm, lbl_ref, hw_ref, cw_ref, ya_ref, yb_ref,
                        stage_ref, wbf_ref, t0_ref, t1_ref, state_ref, p_ref, sem,
                        *, n_heads, n_i, col_offsets):
    s = pl.program_id(0)
    n_steps = n_heads * n_i
    head = jnp.minimum(s // n_i, n_heads - 1)
    copies = functools.partial(_head_weight_copies, w_hbm, stage_ref, sem,
                               col_offsets=col_offsets)

    @pl.when(s == 0)
    def _():
        for c in copies(0):
            c.start()
        t1_ref[...] = jnp.zeros_like(t1_ref)
        state_ref[...] = jnp.zeros_like(state_ref)
        p_ref[...] = jnp.zeros_like(p_ref)

    @pl.when((s % n_i == 0) & (s < n_steps))
    def _():
        for c in copies(head):
            c.wait()
        wbf_ref[...] = stage_ref[...].astype(wbf_ref.dtype)

        @pl.when(head + 1 < n_heads)
        def _():
            for c in copies(head + 1):
                c.start()

    first_block = (jnp.maximum(s - 1, 0) % n_i) == 0
    bm = u_ref.shape[0]
    pieces = [(pl.ds(r, PIECE_ROWS), pl.ds(c, PIECE_COLS))
              for c in range(0, wbf_ref.shape[1], PIECE_COLS) for r in range(0, bm, PIECE_ROWS)]

    def step(t_write, t_read):
        groups, conv = _mixer_stage_groups(t_read, lbl_ref, hw_ref, cw_ref, ya_ref, yb_ref,
                                           state_ref, p_ref, first_block)
        for g, stages in enumerate(groups):
            for stage in stages:
                stage()
            lo, hi = (g * len(pieces)) // len(groups), ((g + 1) * len(pieces)) // len(groups)
            for rows, cols in pieces[lo:hi]:
                t_write[rows, cols] = _dot(u_ref[rows, :], wbf_ref[:, cols])
        conv()

    @pl.when(s % 2 == 0)
    def _():
        step(t0_ref, t1_ref)

    @pl.when(s % 2 == 1)
    def _():
        step(t1_ref, t0_ref)


def _proj_mixers(u, w, lb_logits, hgrn_w_row, conv_taps, *, n_heads, col_offsets, bm):
    t_len, d = u.shape
    n_i = t_len // bm
    n_steps = n_heads * n_i
    n_layers_p1 = lb_logits.shape[0]
    k_taps = conv_taps.shape[0]
    width = n_heads * LANES
    tile_cols = N_PLANES * LANES

    def prev(s):
        return jnp.maximum(s - 1, 0)

    out_spec = pl.BlockSpec((bm, LANES), lambda s: (prev(s) % n_i, prev(s) // n_i))
    return pl.pallas_call(
        functools.partial(_proj_mixers_kernel, n_heads=n_heads, n_i=n_i,
                          col_offsets=tuple(col_offsets)),
        grid=(n_steps + 1,),
        in_specs=[
            pl.BlockSpec((bm, d), lambda s: (jnp.minimum(s, n_steps - 1) % n_i, 0)),
            pl.BlockSpec(memory_space=pl.ANY),
            pl.BlockSpec((n_layers_p1, LANES), lambda s: (0, prev(s) // n_i)),
            pl.BlockSpec((1, LANES), lambda s: (0, 0)),
            pl.BlockSpec((k_taps, LANES), lambda s: (0, prev(s) // n_i)),
        ],
        out_specs=[out_spec, out_spec],
        out_shape=[jax.ShapeDtypeStruct((t_len, width), _bf16)] * 2,
        scratch_shapes=[
            pltpu.VMEM((d, tile_cols), _f32),
            pltpu.VMEM((d, tile_cols), _bf16),
            pltpu.VMEM((bm, tile_cols), _f32),
            pltpu.VMEM((bm, tile_cols), _f32),
            pltpu.VMEM((LANES, LANES), _f32),
            pltpu.VMEM((bm + SUBLANES, LANES), _f32),
            pltpu.SemaphoreType.DMA((N_PLANES,)),
        ],
        compiler_params=pltpu.CompilerParams(
            dimension_semantics=("arbitrary",), vmem_limit_bytes=VMEM_LIMIT_BYTES),
        name="proj_mixers",
    )(u, w, lb_logits, hgrn_w_row, conv_taps)


def _in_proj_kernel(u_ref, w_ref, o_ref, wb_ref):
    @pl.when(pl.program_id(1) == 0)
    def _():
        wb_ref[...] = w_ref[...].astype(wb_ref.dtype)

    o_ref[...] = _dot(u_ref[...], wb_ref[...]).astype(o_ref.dtype)


def _in_proj(u, w, *, col0, n_cols, bm, bn):
    t_len, d = u.shape
    base = col0 // bn
    assert base * bn == col0
    return pl.pallas_call(
        _in_proj_kernel,
        grid=(n_cols // bn, t_len // bm),
        in_specs=[pl.BlockSpec((bm, d), lambda j, i: (i, 0)),
                  pl.BlockSpec((d, bn), lambda j, i: (0, base + j))],
        out_specs=pl.BlockSpec((bm, bn), lambda j, i: (i, j)),
        out_shape=jax.ShapeDtypeStruct((t_len, n_cols), _bf16),
        scratch_shapes=[pltpu.VMEM((d, bn), _bf16)],
        compiler_params=pltpu.CompilerParams(
            dimension_semantics=("parallel", "arbitrary"), vmem_limit_bytes=VMEM_LIMIT_BYTES),
        name="in_proj",
    )(u, w)


def _merge_kernel(ya_ref, yb_ref, ga_ref, gb_ref, bias_ref, wa_ref, wb_ref, wo_ref,
                  o_ref, wo_bf_ref, wa_bf_ref, wb_bf_ref):
    @pl.when(pl.program_id(1) == 0)
    def _():
        wa_bf_ref[...] = wa_ref[...].astype(wa_bf_ref.dtype)
        wb_bf_ref[...] = wb_ref[...].astype(wb_bf_ref.dtype)

    wo_bf_ref[...] = wo_ref[...].astype(wo_bf_ref.dtype)

    za = _dot(ya_ref[...], wa_bf_ref[...])
    zb = _dot(yb_ref[...], wb_bf_ref[...])
    gate_a = _sigmoid(ga_ref[...].astype(_f32) + bias_ref[0:1, :])
    gate_b = _sigmoid(gb_ref[...].astype(_f32) + bias_ref[1:2, :])
    o_ref[...] = (gate_a * za + gate_b * zb).astype(o_ref.dtype)


def _merge(ya, yb, gates, gate_bias, wa, wb, wo, *, ga_off, gb_off, bm, bn):
    t_len, width = ya.shape
    d = wa.shape[1]
    ga_base, gb_base = ga_off // bn, gb_off // bn
    n_i = t_len // bm
    n_steps = (d // bn) * n_i
    wo_rows = wo.shape[0] // n_steps
    assert wo_rows * n_steps == wo.shape[0] and wo_rows % 16 == 0
    return pl.pallas_call(
        _merge_kernel,
        grid=(d // bn, n_i),
        in_specs=[
            pl.BlockSpec((bm, width), lambda n, i: (i, 0)),
            pl.BlockSpec((bm, width), lambda n, i: (i, 0)),
            pl.BlockSpec((bm, bn), lambda n, i: (i, ga_base + n)),
            pl.BlockSpec((bm, bn), lambda n, i: (i, gb_base + n)),
            pl.BlockSpec((2, bn), lambda n, i: (0, n)),
            pl.BlockSpec((width, bn), lambda n, i: (0, n)),
            pl.BlockSpec((width, bn), lambda n, i: (0, n)),
            pl.BlockSpec((wo_rows, wo.shape[1]), lambda n, i: (n * n_i + i, 0)),
        ],
        out_specs=[pl.BlockSpec((bm, bn), lambda n, i: (i, n)),
                   pl.BlockSpec((wo_rows, wo.shape[1]), lambda n, i: (n * n_i + i, 0))],
        out_shape=[jax.ShapeDtypeStruct((t_len, d), _bf16),
                   jax.ShapeDtypeStruct(wo.shape, _bf16)],
        scratch_shapes=[pltpu.VMEM((width, bn), _bf16), pltpu.VMEM((width, bn), _bf16)],
        compiler_params=pltpu.CompilerParams(
            dimension_semantics=("parallel", "arbitrary"), vmem_limit_bytes=VMEM_LIMIT_BYTES),
        name="merge",
    )(ya, yb, gates, gates, gate_bias, wa, wb, wo)


def _out_proj_kernel(m_ref, wo_ref, x_ref, fw_ref, o_ref):
    n = pl.program_id(1)
    bn = x_ref.shape[1]
    col = pl.multiple_of(n * bn, LANES)
    o_ref[:, pl.ds(col, bn)] = x_ref[...] + _dot(m_ref[...], wo_ref[...])

    @pl.when(n == pl.num_programs(1) - 1)
    def _():
        h = o_ref[...]
        ms = jnp.mean(h * h, axis=-1, keepdims=True)
        o_ref[...] = h * lax.rsqrt(ms + EPS) * fw_ref[...]


def _out_proj(merged, wo, x2d, fw_row, *, bm, bn):
    t_len, d = x2d.shape
    return pl.pallas_call(
        _out_proj_kernel,
        grid=(t_len // bm, d // bn),
        in_specs=[
            pl.BlockSpec((bm, d), lambda i, n: (i, 0)),
            pl.BlockSpec((d, bn), lambda i, n: (0, n)),
            pl.BlockSpec((bm, bn), lambda i, n: (i, n)),
            pl.BlockSpec((1, d), lambda i, n: (0, 0)),
        ],
        out_specs=pl.BlockSpec((bm, d), lambda i, n: (i, 0)),
        out_shape=jax.ShapeDtypeStruct((t_len, d), _f32),
        compiler_params=pltpu.CompilerParams(
            dimension_semantics=("parallel", "arbitrary"), vmem_limit_bytes=VMEM_LIMIT_BYTES),
        name="out_proj",
    )(merged, wo, x2d, fw_row)


def kernel(x, norm_w, w_in, lb_logits, hgrn_norm_w, conv_w, w_branch_a, w_branch_b,
           gate_bias, w_out, final_norm_w):
    bsz, t_len, d = x.shape
    assert bsz == 1 and norm_w.shape[0] == 1, "single sequence, single layer"
    key_w = lb_logits.shape[1]
    val_w = w_branch_a.shape[1]
    conv_wd = conv_w.shape[2]
    assert hgrn_norm_w.shape[1] == LANES and key_w == val_w == conv_wd
    n_heads = key_w // LANES
    sizes = (key_w, key_w, val_w, val_w, conv_wd, conv_wd, conv_wd, conv_wd, d, d)
    offs = [sum(sizes[:i]) for i in range(len(sizes))]
    assert w_in.shape[2] == sum(sizes)

    x2d = x.reshape(t_len, d)
    w2d = w_in.reshape(d, w_in.shape[2])
    u = _prenorm(x2d, norm_w, block_rows=min(256, t_len))
    ya, yb = _proj_mixers(u, w2d, lb_logits, hgrn_norm_w, conv_w[0], n_heads=n_heads,
                          col_offsets=offs[:N_PLANES], bm=min(512, t_len))
    gates = _in_proj(u, w2d, col0=offs[8], n_cols=2 * d, bm=min(1024, t_len), bn=min(512, d))
    merged, wo_bf = _merge(ya, yb, gates, gate_bias[0], w_branch_a[0], w_branch_b[0], w_out[0],
                           ga_off=0, gb_off=d, bm=min(512, t_len), bn=min(512, d))
    out = _out_proj(merged, wo_bf, x2d, final_norm_w.reshape(1, d),
                    bm=min(512, t_len), bn=min(512, d))
    return out.reshape(bsz, t_len, d)
```

```python
import functools

import jax
import jax.numpy as jnp
from jax import lax
from jax.experimental import pallas as pl
from jax.experimental.pallas import tpu as pltpu

EPS = 1e-6
LANES = 128
SUBLANES = 8
CHUNK = 128
DIAG_BLOCK = 16
N_PLANES = 8
PIECE_ROWS = 128
PIECE_COLS = 256
MERGE_PIECE_ROWS = 256
CAST_ROWS = 32
VMEM_LIMIT_BYTES = 56 * 1024 * 1024

_f32 = jnp.float32
_bf16 = jnp.bfloat16


def _sigmoid(v):
    return 0.5 * jnp.tanh(0.5 * v) + 0.5


def _silu(v):
    h = 0.5 * v
    return h * jnp.tanh(h) + h


def _dot(a, b):
    return jnp.dot(a, b, preferred_element_type=_f32)


def _dot_nt(a, b):
    return lax.dot_general(a, b, (((1,), (1,)), ((), ())), preferred_element_type=_f32)


def _dot_tn(a, b):
    return lax.dot_general(a, b, (((0,), (0,)), ((), ())), preferred_element_type=_f32)


def _prenorm_kernel(x_ref, w_ref, o_ref):
    x = x_ref[...]
    ms = jnp.mean(x * x, axis=-1, keepdims=True)
    o_ref[...] = (x * lax.rsqrt(ms + EPS) * w_ref[...]).astype(o_ref.dtype)


def _prenorm(x2d, w_row, *, block_rows):
    t_len, d = x2d.shape
    return pl.pallas_call(
        _prenorm_kernel,
        grid=(t_len // block_rows,),
        in_specs=[pl.BlockSpec((block_rows, d), lambda i: (i, 0)),
                  pl.BlockSpec((1, d), lambda i: (0, 0))],
        out_specs=pl.BlockSpec((block_rows, d), lambda i: (i, 0)),
        out_shape=jax.ShapeDtypeStruct((t_len, d), _bf16),
        compiler_params=pltpu.CompilerParams(
            dimension_semantics=("parallel",), vmem_limit_bytes=VMEM_LIMIT_BYTES),
        name="prenorm",
    )(x2d, w_row)


def _row_block_reference(b, pair_rows, ref_row):
    c = b.shape[0]
    grouped = b.reshape(c // pair_rows, pair_rows, LANES)
    picked = grouped[:, ref_row:ref_row + 1, :]
    return jnp.broadcast_to(picked, grouped.shape).reshape(c, LANES)


def _chunk_gates(a_f, lb, tri):
    f = lb + (1.0 - lb) * _sigmoid(a_f)
    g = jnp.log(f)
    g_hi = g.astype(_bf16)
    g_lo = (g - g_hi.astype(_f32)).astype(_bf16)
    return 1.0 - f, _dot(tri, jnp.concatenate([g_hi, g_lo], axis=1))


def _chunk_scores(q, k, b2):
    c = q.shape[0]
    b = b2[:, :LANES] + b2[:, LANES:]
    ref = _row_block_reference(b, DIAG_BLOCK, DIAG_BLOCK // 2 - 1)
    raw = [_dot_nt((q * jnp.exp(b - ref)).astype(_bf16), (k * jnp.exp(ref - b)).astype(_bf16))]
    m = DIAG_BLOCK
    while m < c:
        ref = _row_block_reference(b, 2 * m, m - 1)
        e = jnp.exp(-jnp.abs(b - ref))
        raw.append(_dot_nt((q * e).astype(_bf16), (k * e).astype(_bf16)))
        m *= 2
    b_last = b[c - 1:c, :]
    qe = (q * jnp.exp(b)).astype(_bf16)
    ke = (k * jnp.exp(b_last - b)).astype(_bf16)
    return raw, qe, ke, jnp.exp(b_last)


def _chunk_output(raw, qe, ke, decay, v, state_t):
    c = qe.shape[0]
    rows = lax.broadcasted_iota(jnp.int32, (c, c), 0)
    cols = lax.broadcasted_iota(jnp.int32, (c, c), 1)
    shift = DIAG_BLOCK.bit_length() - 1
    mask = ((rows >> shift) == (cols >> shift)) & (cols <= rows)
    scores = jnp.where(mask, raw[0], 0.0)
    for lvl, s_lvl in enumerate(raw[1:]):
        rblk = rows >> (shift + lvl)
        mask = ((rblk & 1) == 1) & ((cols >> (shift + lvl)) == rblk - 1)
        scores = jnp.where(mask, s_lvl, scores)
    o = _dot(scores.astype(_bf16), v) + _dot_nt(qe, state_t.astype(_bf16))
    return o, state_t * decay + _dot_tn(v, ke)


def _mixer_stage_groups(t_ref, lbl_ref, hw_ref, cw_ref, ya_ref, yb_ref, state_ref, p_ref,
                        first_block):
    tb = t_ref.shape[0]
    n_chunks = tb // CHUNK

    def plane(idx, rs=slice(None)):
        return t_ref[rs, idx * LANES:(idx + 1) * LANES]

    lbl = lbl_ref[...]
    ex = jnp.exp(lbl - jnp.max(lbl, axis=0, keepdims=True))
    lb = ex[0:1, :] / jnp.sum(ex, axis=0, keepdims=True)
    tri = (lax.broadcasted_iota(jnp.int32, (CHUNK, CHUNK), 1)
           <= lax.broadcasted_iota(jnp.int32, (CHUNK, CHUNK), 0)).astype(_bf16)
    hw = hw_ref[...]

    live = {"state": jnp.where(first_block, 0.0, state_ref[...])}

    def rows_of(c):
        return pl.ds(c * CHUNK, CHUNK)

    def gates(c):
        live[c] = _chunk_gates(plane(1, rows_of(c)), lb, tri)

    def scores(c):
        k, b2 = live[c]
        live[c] = _chunk_scores(plane(0, rows_of(c)), k, b2)

    def output(c):
        rs = rows_of(c)
        o, live["state"] = _chunk_output(*live.pop(c), plane(2, rs).astype(_bf16), live["state"])
        ms = jnp.mean(o * o, axis=-1, keepdims=True)
        gate = plane(3, rs)
        ya = (o * lax.rsqrt(ms + EPS) * hw) * _silu(gate)
        ya_ref[rs, :] = ya.astype(ya_ref.dtype)
        if c == n_chunks - 1:
            state_ref[...] = live["state"]

    groups = []
    for g in range(n_chunks + 2):
        stages = [(output, g - 2), (scores, g - 1), (gates, g)]
        groups.append([functools.partial(fn, c) for fn, c in stages if 0 <= c < n_chunks])

    def conv():
        p_ref[0:SUBLANES, :] = jnp.where(first_block, 0.0, p_ref[tb:tb + SUBLANES, :])
        p_ref[SUBLANES:SUBLANES + tb, :] = plane(5) * plane(6)
        k_taps = cw_ref.shape[0]
        acc = None
        for j in range(k_taps):
            off = SUBLANES - (k_taps - 1) + j
            term = cw_ref[j:j + 1, :] * p_ref[off:off + tb, :]
            acc = term if acc is None else acc + term
        yb = (plane(4) * acc) * _silu(plane(7))
        yb_ref[...] = yb.astype(yb_ref.dtype)

    return groups, conv


def _stage_weight_tile(copies_of, tile, n_tiles, stage_ref, wbf_ref):
    for c in copies_of(tile):
        c.wait()

    def cast_rows(r, carry):
        rows = pl.ds(pl.multiple_of(r * CAST_ROWS, CAST_ROWS), CAST_ROWS)
        wbf_ref[rows, :] = stage_ref[rows, :].astype(wbf_ref.dtype)
        return carry

    lax.fori_loop(0, stage_ref.shape[0] // CAST_ROWS, cast_rows, 0)

    @pl.when(tile + 1 < n_tiles)
    def _():
        for c in copies_of(tile + 1):
            c.start()


def _head_weight_copies(w_hbm, stage_ref, sem, head, col_offsets):
    return [pltpu.make_async_copy(
        w_hbm.at[:, pl.ds(pl.multiple_of(off + head * LANES, LANES), LANES)],
        stage_ref.at[:, pl.ds(idx * LANES, LANES)], sem.at[idx])
        for idx, off in enumerate(col_offsets)]


def _proj_mixers_kernel(u_ref, w_hbm, lbl_ref, hw_ref, cw_ref, ya_ref, yb_ref,
                        stage_ref, wbf_ref, t0_ref, t1_ref, state_ref, p_ref, sem,
                        *, n_heads, n_i, col_offsets):
    s = pl.program_id(0)
    n_steps = n_heads * n_i
    head = jnp.minimum(s // n_i, n_heads - 1)
    copies = functools.partial(_head_weight_copies, w_hbm, stage_ref, sem,
                               col_offsets=col_offsets)

    @pl.when(s == 0)
    def _():
        for c in copies(0):
            c.start()
        t1_ref[...] = jnp.zeros_like(t1_ref)
        state_ref[...] = jnp.zeros_like(state_ref)
        p_ref[...] = jnp.zeros_like(p_ref)

    @pl.when((s % n_i == 0) & (s < n_steps))
    def _():
        _stage_weight_tile(copies, head, n_heads, stage_ref, wbf_ref)

    first_block = (jnp.maximum(s - 1, 0) % n_i) == 0
    bm = u_ref.shape[0]
    pieces = [(pl.ds(r, PIECE_ROWS), pl.ds(c, PIECE_COLS))
              for c in range(0, wbf_ref.shape[1], PIECE_COLS) for r in range(0, bm, PIECE_ROWS)]

    def step(t_write, t_read):
        groups, conv = _mixer_stage_groups(t_read, lbl_ref, hw_ref, cw_ref, ya_ref, yb_ref,
                                           state_ref, p_ref, first_block)
        for g, stages in enumerate(groups):
            for stage in stages:
                stage()
            lo, hi = (g * len(pieces)) // len(groups), ((g + 1) * len(pieces)) // len(groups)
            for rows, cols in pieces[lo:hi]:
                t_write[rows, cols] = _dot(u_ref[rows, :], wbf_ref[:, cols])
        conv()

    @pl.when(s % 2 == 0)
    def _():
        step(t0_ref, t1_ref)

    @pl.when(s % 2 == 1)
    def _():
        step(t1_ref, t0_ref)


def _proj_mixers(u, w, lb_logits, hgrn_w_row, conv_taps, *, n_heads, col_offsets, bm):
    t_len, d = u.shape
    n_i = t_len // bm
    n_steps = n_heads * n_i
    n_layers_p1 = lb_logits.shape[0]
    k_taps = conv_taps.shape[0]
    width = n_heads * LANES
    tile_cols = N_PLANES * LANES

    def prev(s):
        return jnp.maximum(s - 1, 0)

    out_spec = pl.BlockSpec((bm, LANES), lambda s: (prev(s) % n_i, prev(s) // n_i))
    return pl.pallas_call(
        functools.partial(_proj_mixers_kernel, n_heads=n_heads, n_i=n_i,
                          col_offsets=tuple(col_offsets)),
        grid=(n_steps + 1,),
        in_specs=[
            pl.BlockSpec((bm, d), lambda s: (jnp.minimum(s, n_steps - 1) % n_i, 0)),
            pl.BlockSpec(memory_space=pl.ANY),
            pl.BlockSpec((n_layers_p1, LANES), lambda s: (0, prev(s) // n_i)),
            pl.BlockSpec((1, LANES), lambda s: (0, 0)),
            pl.BlockSpec((k_taps, LANES), lambda s: (0, prev(s) // n_i)),
        ],
        out_specs=[out_spec, out_spec],
        out_shape=[jax.ShapeDtypeStruct((t_len, width), _bf16)] * 2,
        scratch_shapes=[
            pltpu.VMEM((d, tile_cols), _f32),
            pltpu.VMEM((d, tile_cols), _bf16),
            pltpu.VMEM((bm, tile_cols), _f32),
            pltpu.VMEM((bm, tile_cols), _f32),
            pltpu.VMEM((LANES, LANES), _f32),
            pltpu.VMEM((bm + SUBLANES, LANES), _f32),
            pltpu.SemaphoreType.DMA((N_PLANES,)),
        ],
        compiler_params=pltpu.CompilerParams(
            dimension_semantics=("arbitrary",), vmem_limit_bytes=VMEM_LIMIT_BYTES),
        name="proj_mixers",
    )(u, w, lb_logits, hgrn_w_row, conv_taps)


def _column_pieces(n_cols):
    width = min(PIECE_COLS, n_cols)
    return [pl.ds(c, width) for c in range(0, n_cols, width)]


def _in_proj_kernel(u_ref, w_hbm, o_ref, stage_ref, wbf_ref, sem, *, n_tiles, n_i, col0):
    s = pl.program_id(0)
    bn = wbf_ref.shape[1]

    def copies(tile):
        cols = pl.ds(pl.multiple_of(col0 + tile * bn, LANES), bn)
        return [pltpu.make_async_copy(w_hbm.at[:, cols], stage_ref, sem.at[0])]

    @pl.when(s == 0)
    def _():
        for c in copies(0):
            c.start()

    @pl.when(s % n_i == 0)
    def _():
        _stage_weight_tile(copies, s // n_i, n_tiles, stage_ref, wbf_ref)

    for cols in _column_pieces(bn):
        o_ref[:, cols] = _dot(u_ref[...], wbf_ref[:, cols]).astype(o_ref.dtype)


def _in_proj(u, w, *, col0, n_cols, bm, bn):
    t_len, d = u.shape
    n_i, n_tiles = t_len // bm, n_cols // bn
    return pl.pallas_call(
        functools.partial(_in_proj_kernel, n_tiles=n_tiles, n_i=n_i, col0=col0),
        grid=(n_tiles * n_i,),
        in_specs=[pl.BlockSpec((bm, d), lambda s: (s % n_i, 0)),
                  pl.BlockSpec(memory_space=pl.ANY)],
        out_specs=pl.BlockSpec((bm, bn), lambda s: (s % n_i, s // n_i)),
        out_shape=jax.ShapeDtypeStruct((t_len, n_cols), _bf16),
        scratch_shapes=[pltpu.VMEM((d, bn), _f32), pltpu.VMEM((d, bn), _bf16),
                        pltpu.SemaphoreType.DMA((1,))],
        compiler_params=pltpu.CompilerParams(
            dimension_semantics=("arbitrary",), vmem_limit_bytes=VMEM_LIMIT_BYTES),
        name="in_proj",
    )(u, w)


def _merge_kernel(ya_ref, yb_ref, ga_ref, gb_ref, bias_ref, wa_ref, wb_ref, wo_ref,
                  o_ref, wo_bf_ref, wa_bf_ref, wb_bf_ref):
    @pl.when(pl.program_id(1) == 0)
    def _():
        wa_bf_ref[...] = wa_ref[...].astype(wa_bf_ref.dtype)
        wb_bf_ref[...] = wb_ref[...].astype(wb_bf_ref.dtype)

    wo_bf_ref[...] = wo_ref[...].astype(wo_bf_ref.dtype)

    for cols in _column_pieces(o_ref.shape[1]):
        for r in range(0, o_ref.shape[0], MERGE_PIECE_ROWS):
            rows = pl.ds(r, MERGE_PIECE_ROWS)
            za = _dot(ya_ref[rows, :], wa_bf_ref[:, cols])
            zb = _dot(yb_ref[rows, :], wb_bf_ref[:, cols])
            gate_a = _sigmoid(ga_ref[rows, cols].astype(_f32) + bias_ref[0:1, cols])
            gate_b = _sigmoid(gb_ref[rows, cols].astype(_f32) + bias_ref[1:2, cols])
            o_ref[rows, cols] = (gate_a * za + gate_b * zb).astype(o_ref.dtype)


def _merge(ya, yb, gates, gate_bias, wa, wb, wo, *, ga_off, gb_off, bm, bn):
    t_len, width = ya.shape
    d = wa.shape[1]
    ga_base, gb_base = ga_off // bn, gb_off // bn
    n_i = t_len // bm
    n_steps = (d // bn) * n_i
    wo_rows = wo.shape[0] // n_steps
    assert wo_rows * n_steps == wo.shape[0] and wo_rows % 16 == 0
    return pl.pallas_call(
        _merge_kernel,
        grid=(d // bn, n_i),
        in_specs=[
            pl.BlockSpec((bm, width), lambda n, i: (i, 0)),
            pl.BlockSpec((bm, width), lambda n, i: (i, 0)),
            pl.BlockSpec((bm, bn), lambda n, i: (i, ga_base + n)),
            pl.BlockSpec((bm, bn), lambda n, i: (i, gb_base + n)),
            pl.BlockSpec((2, bn), lambda n, i: (0, n)),
            pl.BlockSpec((width, bn), lambda n, i: (0, n)),
            pl.BlockSpec((width, bn), lambda n, i: (0, n)),
            pl.BlockSpec((wo_rows, wo.shape[1]), lambda n, i: (n * n_i + i, 0)),
        ],
        out_specs=[pl.BlockSpec((bm, bn), lambda n, i: (i, n)),
                   pl.BlockSpec((wo_rows, wo.shape[1]), lambda n, i: (n * n_i + i, 0))],
        out_shape=[jax.ShapeDtypeStruct((t_len, d), _bf16),
                   jax.ShapeDtypeStruct(wo.shape, _bf16)],
        scratch_shapes=[pltpu.VMEM((width, bn), _bf16), pltpu.VMEM((width, bn), _bf16)],
        compiler_params=pltpu.CompilerParams(
            dimension_semantics=("parallel", "arbitrary"), vmem_limit_bytes=VMEM_LIMIT_BYTES),
        name="merge",
    )(ya, yb, gates, gates, gate_bias, wa, wb, wo)


def _out_proj_kernel(m_ref, wo_ref, x_ref, fw_ref, o_ref):
    n = pl.program_id(1)
    bn = x_ref.shape[1]
    for cols in _column_pieces(bn):
        out_cols = pl.ds(pl.multiple_of(n * bn + cols.start, LANES), cols.size)
        o_ref[:, out_cols] = x_ref[:, cols] + _dot(m_ref[...], wo_ref[:, cols])

    @pl.when(n == pl.num_programs(1) - 1)
    def _():
        h = o_ref[...]
        ms = jnp.mean(h * h, axis=-1, keepdims=True)
        o_ref[...] = h * lax.rsqrt(ms + EPS) * fw_ref[...]


def _out_proj(merged, wo, x2d, fw_row, *, bm, bn):
    t_len, d = x2d.shape
    return pl.pallas_call(
        _out_proj_kernel,
        grid=(t_len // bm, d // bn),
        in_specs=[
            pl.BlockSpec((bm, d), lambda i, n: (i, 0)),
            pl.BlockSpec((d, bn), lambda i, n: (0, n)),
            pl.BlockSpec((bm, bn), lambda i, n: (i, n)),
            pl.BlockSpec((1, d), lambda i, n: (0, 0)),
        ],
        out_specs=pl.BlockSpec((bm, d), lambda i, n: (i, 0)),
        out_shape=jax.ShapeDtypeStruct((t_len, d), _f32),
        compiler_params=pltpu.CompilerParams(
            dimension_semantics=("parallel", "arbitrary"), vmem_limit_bytes=VMEM_LIMIT_BYTES),
        name="out_proj",
    )(merged, wo, x2d, fw_row)


def kernel(x, norm_w, w_in, lb_logits, hgrn_norm_w, conv_w, w_branch_a, w_branch_b,
           gate_bias, w_out, final_norm_w):
    bsz, t_len, d = x.shape
    assert bsz == 1 and norm_w.shape[0] == 1, "single sequence, single layer"
    key_w = lb_logits.shape[1]
    val_w = w_branch_a.shape[1]
    conv_wd = conv_w.shape[2]
    assert hgrn_norm_w.shape[1] == LANES and key_w == val_w == conv_wd
    n_heads = key_w // LANES
    sizes = (key_w, key_w, val_w, val_w, conv_wd, conv_wd, conv_wd, conv_wd, d, d)
    offs = [sum(sizes[:i]) for i in range(len(sizes))]
    assert w_in.shape[2] == sum(sizes)

    x2d = x.reshape(t_len, d)
    w2d = w_in.reshape(d, w_in.shape[2])
    u = _prenorm(x2d, norm_w, block_rows=min(256, t_len))
    ya, yb = _proj_mixers(u, w2d, lb_logits, hgrn_norm_w, conv_w[0], n_heads=n_heads,
                          col_offsets=offs[:N_PLANES], bm=min(1024, t_len))
    gates = _in_proj(u, w2d, col0=offs[8], n_cols=2 * d, bm=min(1024, t_len), bn=min(1024, d))
    merged, wo_bf = _merge(ya, yb, gates, gate_bias[0], w_branch_a[0], w_branch_b[0], w_out[0],
                           ga_off=0, gb_off=d, bm=min(1024, t_len), bn=min(512, d))
    out = _out_proj(merged, wo_bf, x2d, final_norm_w.reshape(1, d),
                    bm=min(512, t_len), bn=min(1024, d))
    return out.reshape(bsz, t_len, d)
```

```python
import functools

import jax
import jax.numpy as jnp
from jax import lax
from jax.experimental import pallas as pl
from jax.experimental.pallas import tpu as pltpu

EPS = 1e-6
LANES = 128
SUBLANES = 8
CHUNK = 128
DIAG_BLOCK = 16
N_PLANES = 8
PIECE_ROWS = 128
PIECE_COLS = 512
COLUMN_PIECE = 512
MERGE_PIECE_ROWS = 256
CAST_ROWS = 32
VMEM_LIMIT_BYTES = 56 * 1024 * 1024

_f32 = jnp.float32
_bf16 = jnp.bfloat16


def _sigmoid(v):
    return 0.5 * jnp.tanh(0.5 * v) + 0.5


def _silu(v):
    h = 0.5 * v
    return h * jnp.tanh(h) + h


def _dot(a, b):
    return jnp.dot(a, b, preferred_element_type=_f32)


def _dot_nt(a, b):
    return lax.dot_general(a, b, (((1,), (1,)), ((), ())), preferred_element_type=_f32)


def _dot_tn(a, b):
    return lax.dot_general(a, b, (((0,), (0,)), ((), ())), preferred_element_type=_f32)


def _prenorm_kernel(x_ref, w_ref, o_ref):
    x = x_ref[...]
    ms = jnp.mean(x * x, axis=-1, keepdims=True)
    o_ref[...] = (x * lax.rsqrt(ms + EPS) * w_ref[...]).astype(o_ref.dtype)


def _prenorm(x2d, w_row, *, block_rows):
    t_len, d = x2d.shape
    return pl.pallas_call(
        _prenorm_kernel,
        grid=(t_len // block_rows,),
        in_specs=[pl.BlockSpec((block_rows, d), lambda i: (i, 0)),
                  pl.BlockSpec((1, d), lambda i: (0, 0))],
        out_specs=pl.BlockSpec((block_rows, d), lambda i: (i, 0)),
        out_shape=jax.ShapeDtypeStruct((t_len, d), _bf16),
        compiler_params=pltpu.CompilerParams(
            dimension_semantics=("parallel",), vmem_limit_bytes=VMEM_LIMIT_BYTES),
        name="prenorm",
    )(x2d, w_row)


def _row_block_reference(b, pair_rows, ref_row):
    c = b.shape[0]
    grouped = b.reshape(c // pair_rows, pair_rows, LANES)
    picked = grouped[:, ref_row:ref_row + 1, :]
    return jnp.broadcast_to(picked, grouped.shape).reshape(c, LANES)


def _chunk_gates(a_f, lb, tri):
    f = lb + (1.0 - lb) * _sigmoid(a_f)
    g = jnp.log2(f)
    g_hi = g.astype(_bf16)
    g_lo = (g - g_hi.astype(_f32)).astype(_bf16)
    return 1.0 - f, _dot(tri, jnp.concatenate([g_hi, g_lo], axis=1))


def _chunk_scores(q, k, b2):
    c = q.shape[0]
    b = b2[:, :LANES] + b2[:, LANES:]
    ref = _row_block_reference(b, DIAG_BLOCK, DIAG_BLOCK // 2 - 1)
    raw = [_dot_nt((q * jnp.exp2(b - ref)).astype(_bf16), (k * jnp.exp2(ref - b)).astype(_bf16))]
    m = DIAG_BLOCK
    while m < c:
        ref = _row_block_reference(b, 2 * m, m - 1)
        e = jnp.exp2(-jnp.abs(b - ref))
        raw.append(_dot_nt((q * e).astype(_bf16), (k * e).astype(_bf16)))
        m *= 2
    b_last = b[c - 1:c, :]
    qe = (q * jnp.exp2(b)).astype(_bf16)
    ke = (k * jnp.exp2(b_last - b)).astype(_bf16)
    return raw, qe, ke, jnp.exp2(b_last)


def _chunk_output(raw, qe, ke, decay, v, state_t):
    c = qe.shape[0]
    rows = lax.broadcasted_iota(jnp.int32, (c, c), 0)
    cols = lax.broadcasted_iota(jnp.int32, (c, c), 1)
    shift = DIAG_BLOCK.bit_length() - 1
    mask = ((rows >> shift) == (cols >> shift)) & (cols <= rows)
    scores = jnp.where(mask, raw[0], 0.0)
    for lvl, s_lvl in enumerate(raw[1:]):
        rblk = rows >> (shift + lvl)
        mask = ((rblk & 1) == 1) & ((cols >> (shift + lvl)) == rblk - 1)
        scores = jnp.where(mask, s_lvl, scores)
    o = _dot(scores.astype(_bf16), v) + _dot_nt(qe, state_t.astype(_bf16))
    return o, state_t * decay + _dot_tn(v, ke)


def _mixer_stage_groups(t_ref, lbl_ref, hw_ref, cw_ref, ya_ref, yb_ref, state_ref, p_ref,
                        first_block):
    tb = t_ref.shape[0]
    n_chunks = tb // CHUNK

    def plane(idx, rs=slice(None)):
        return t_ref[rs, idx * LANES:(idx + 1) * LANES]

    lbl = lbl_ref[...]
    ex = jnp.exp(lbl - jnp.max(lbl, axis=0, keepdims=True))
    lb = ex[0:1, :] / jnp.sum(ex, axis=0, keepdims=True)
    tri = (lax.broadcasted_iota(jnp.int32, (CHUNK, CHUNK), 1)
           <= lax.broadcasted_iota(jnp.int32, (CHUNK, CHUNK), 0)).astype(_bf16)
    hw = hw_ref[...]

    live = {"state": jnp.where(first_block, 0.0, state_ref[...])}

    def rows_of(c):
        return pl.ds(c * CHUNK, CHUNK)

    def gates(c):
        live[c] = _chunk_gates(plane(1, rows_of(c)), lb, tri)

    def scores(c):
        k, b2 = live[c]
        live[c] = _chunk_scores(plane(0, rows_of(c)), k, b2)

    def output(c):
        rs = rows_of(c)
        o, live["state"] = _chunk_output(*live.pop(c), plane(2, rs).astype(_bf16), live["state"])
        ms = jnp.mean(o * o, axis=-1, keepdims=True)
        gate = plane(3, rs)
        ya = (o * lax.rsqrt(ms + EPS) * hw) * _silu(gate)
        ya_ref[rs, :] = ya.astype(ya_ref.dtype)
        if c == n_chunks - 1:
            state_ref[...] = live["state"]

    groups = []
    for g in range(n_chunks + 2):
        stages = [(output, g - 2), (scores, g - 1), (gates, g)]
        groups.append([functools.partial(fn, c) for fn, c in stages if 0 <= c < n_chunks])

    def conv():
        p_ref[0:SUBLANES, :] = jnp.where(first_block, 0.0, p_ref[tb:tb + SUBLANES, :])
        p_ref[SUBLANES:SUBLANES + tb, :] = plane(5) * plane(6)
        k_taps = cw_ref.shape[0]
        acc = None
        for j in range(k_taps):
            off = SUBLANES - (k_taps - 1) + j
            term = cw_ref[j:j + 1, :] * p_ref[off:off + tb, :]
            acc = term if acc is None else acc + term
        yb = (plane(4) * acc) * _silu(plane(7))
        yb_ref[...] = yb.astype(yb_ref.dtype)

    return groups, conv


def _stage_weight_tile(copies_of, tile, n_tiles, stage_ref, wbf_ref):
    for c in copies_of(tile):
        c.wait()

    def cast_rows(r, carry):
        rows = pl.ds(pl.multiple_of(r * CAST_ROWS, CAST_ROWS), CAST_ROWS)
        wbf_ref[rows, :] = stage_ref[rows, :].astype(wbf_ref.dtype)
        return carry

    lax.fori_loop(0, stage_ref.shape[0] // CAST_ROWS, cast_rows, 0)

    @pl.when(tile + 1 < n_tiles)
    def _():
        for c in copies_of(tile + 1):
            c.start()


def _head_weight_copies(w_hbm, stage_ref, sem, head, col_offsets):
    return [pltpu.make_async_copy(
        w_hbm.at[:, pl.ds(pl.multiple_of(off + head * LANES, LANES), LANES)],
        stage_ref.at[:, pl.ds(idx * LANES, LANES)], sem.at[idx])
        for idx, off in enumerate(col_offsets)]


def _proj_mixers_kernel(u_ref, w_hbm, lbl_ref, hw_ref, cw_ref, ya_ref, yb_ref,
                        stage_ref, wbf_ref, t0_ref, t1_ref, state_ref, p_ref, sem,
                        *, n_heads, n_i, col_offsets):
    s = pl.program_id(0)
    n_steps = n_heads * n_i
    head = jnp.minimum(s // n_i, n_heads - 1)
    copies = functools.partial(_head_weight_copies, w_hbm, stage_ref, sem,
                               col_offsets=col_offsets)

    @pl.when(s == 0)
    def _():
        for c in copies(0):
            c.start()
        t1_ref[...] = jnp.zeros_like(t1_ref)
        state_ref[...] = jnp.zeros_like(state_ref)
        p_ref[...] = jnp.zeros_like(p_ref)

    @pl.when((s % n_i == 0) & (s < n_steps))
    def _():
        _stage_weight_tile(copies, head, n_heads, stage_ref, wbf_ref)

    first_block = (jnp.maximum(s - 1, 0) % n_i) == 0
    bm = u_ref.shape[0]
    pieces = [(pl.ds(r, PIECE_ROWS), pl.ds(c, PIECE_COLS))
              for c in range(0, wbf_ref.shape[1], PIECE_COLS) for r in range(0, bm, PIECE_ROWS)]

    def step(t_write, t_read):
        groups, conv = _mixer_stage_groups(t_read, lbl_ref, hw_ref, cw_ref, ya_ref, yb_ref,
                                           state_ref, p_ref, first_block)
        for g, stages in enumerate(groups):
            for stage in stages:
                stage()
            lo, hi = (g * len(pieces)) // len(groups), ((g + 1) * len(pieces)) // len(groups)
            for rows, cols in pieces[lo:hi]:
                t_write[rows, cols] = _dot(u_ref[rows, :], wbf_ref[:, cols])
        conv()

    @pl.when(s % 2 == 0)
    def _():
        step(t0_ref, t1_ref)

    @pl.when(s % 2 == 1)
    def _():
        step(t1_ref, t0_ref)


def _proj_mixers(u, w, lb_logits, hgrn_w_row, conv_taps, *, n_heads, col_offsets, bm):
    t_len, d = u.shape
    n_i = t_len // bm
    n_steps = n_heads * n_i
    n_layers_p1 = lb_logits.shape[0]
    k_taps = conv_taps.shape[0]
    width = n_heads * LANES
    tile_cols = N_PLANES * LANES

    def prev(s):
        return jnp.maximum(s - 1, 0)

    out_spec = pl.BlockSpec((bm, LANES), lambda s: (prev(s) % n_i, prev(s) // n_i))
    return pl.pallas_call(
        functools.partial(_proj_mixers_kernel, n_heads=n_heads, n_i=n_i,
                          col_offsets=tuple(col_offsets)),
        grid=(n_steps + 1,),
        in_specs=[
            pl.BlockSpec((bm, d), lambda s: (jnp.minimum(s, n_steps - 1) % n_i, 0)),
            pl.BlockSpec(memory_space=pl.ANY),
            pl.BlockSpec((n_layers_p1, LANES), lambda s: (0, prev(s) // n_i)),
            pl.BlockSpec((1, LANES), lambda s: (0, 0)),
            pl.BlockSpec((k_taps, LANES), lambda s: (0, prev(s) // n_i)),
        ],
        out_specs=[out_spec, out_spec],
        out_shape=[jax.ShapeDtypeStruct((t_len, width), _bf16)] * 2,
        scratch_shapes=[
            pltpu.VMEM((d, tile_cols), _f32),
            pltpu.VMEM((d, tile_cols), _bf16),
            pltpu.VMEM((bm, tile_cols), _f32),
            pltpu.VMEM((bm, tile_cols), _f32),
            pltpu.VMEM((LANES, LANES), _f32),
            pltpu.VMEM((bm + SUBLANES, LANES), _f32),
            pltpu.SemaphoreType.DMA((N_PLANES,)),
        ],
        compiler_params=pltpu.CompilerParams(
            dimension_semantics=("arbitrary",), vmem_limit_bytes=VMEM_LIMIT_BYTES),
        name="proj_mixers",
    )(u, w, lb_logits, hgrn_w_row, conv_taps)


def _column_pieces(n_cols):
    width = min(COLUMN_PIECE, n_cols)
    return [pl.ds(c, width) for c in range(0, n_cols, width)]


def _in_proj_kernel(u_ref, w_hbm, o_ref, stage_ref, wbf_ref, sem, *, n_tiles, n_i, col0):
    s = pl.program_id(0)
    bn = wbf_ref.shape[1]

    def copies(tile):
        cols = pl.ds(pl.multiple_of(col0 + tile * bn, LANES), bn)
        return [pltpu.make_async_copy(w_hbm.at[:, cols], stage_ref, sem.at[0])]

    @pl.when(s == 0)
    def _():
        for c in copies(0):
            c.start()

    @pl.when(s % n_i == 0)
    def _():
        _stage_weight_tile(copies, s // n_i, n_tiles, stage_ref, wbf_ref)

    for cols in _column_pieces(bn):
        o_ref[:, cols] = _dot(u_ref[...], wbf_ref[:, cols]).astype(o_ref.dtype)


def _in_proj(u, w, *, col0, n_cols, bm, bn):
    t_len, d = u.shape
    n_i, n_tiles = t_len // bm, n_cols // bn
    return pl.pallas_call(
        functools.partial(_in_proj_kernel, n_tiles=n_tiles, n_i=n_i, col0=col0),
        grid=(n_tiles * n_i,),
        in_specs=[pl.BlockSpec((bm, d), lambda s: (s % n_i, 0)),
                  pl.BlockSpec(memory_space=pl.ANY)],
        out_specs=pl.BlockSpec((bm, bn), lambda s: (s % n_i, s // n_i)),
        out_shape=jax.ShapeDtypeStruct((t_len, n_cols), _bf16),
        scratch_shapes=[pltpu.VMEM((d, bn), _f32), pltpu.VMEM((d, bn), _bf16),
                        pltpu.SemaphoreType.DMA((1,))],
        compiler_params=pltpu.CompilerParams(
            dimension_semantics=("arbitrary",), vmem_limit_bytes=VMEM_LIMIT_BYTES),
        name="in_proj",
    )(u, w)


def _merge_kernel(ya_ref, yb_ref, ga_ref, gb_ref, bias_ref, wa_ref, wb_ref, wo_ref,
                  o_ref, wo_bf_ref, wa_bf_ref, wb_bf_ref):
    @pl.when(pl.program_id(1) == 0)
    def _():
        wa_bf_ref[...] = wa_ref[...].astype(wa_bf_ref.dtype)
        wb_bf_ref[...] = wb_ref[...].astype(wb_bf_ref.dtype)

    wo_bf_ref[...] = wo_ref[...].astype(wo_bf_ref.dtype)

    for cols in _column_pieces(o_ref.shape[1]):
        for r in range(0, o_ref.shape[0], MERGE_PIECE_ROWS):
            rows = pl.ds(r, MERGE_PIECE_ROWS)
            za = _dot(ya_ref[rows, :], wa_bf_ref[:, cols])
            zb = _dot(yb_ref[rows, :], wb_bf_ref[:, cols])
            gate_a = _sigmoid(ga_ref[rows, cols].astype(_f32) + bias_ref[0:1, cols])
            gate_b = _sigmoid(gb_ref[rows, cols].astype(_f32) + bias_ref[1:2, cols])
            o_ref[rows, cols] = (gate_a * za + gate_b * zb).astype(o_ref.dtype)


def _merge(ya, yb, gates, gate_bias, wa, wb, wo, *, ga_off, gb_off, bm, bn):
    t_len, width = ya.shape
    d = wa.shape[1]
    ga_base, gb_base = ga_off // bn, gb_off // bn
    n_i = t_len // bm
    n_steps = (d // bn) * n_i
    wo_rows = wo.shape[0] // n_steps
    assert wo_rows * n_steps == wo.shape[0] and wo_rows % 16 == 0
    return pl.pallas_call(
        _merge_kernel,
        grid=(d // bn, n_i),
        in_specs=[
            pl.BlockSpec((bm, width), lambda n, i: (i, 0)),
            pl.BlockSpec((bm, width), lambda n, i: (i, 0)),
            pl.BlockSpec((bm, bn), lambda n, i: (i, ga_base + n)),
            pl.BlockSpec((bm, bn), lambda n, i: (i, gb_base + n)),
            pl.BlockSpec((2, bn), lambda n, i: (0, n)),
            pl.BlockSpec((width, bn), lambda n, i: (0, n)),
            pl.BlockSpec((width, bn), lambda n, i: (0, n)),
            pl.BlockSpec((wo_rows, wo.shape[1]), lambda n, i: (n * n_i + i, 0)),
        ],
        out_specs=[pl.BlockSpec((bm, bn), lambda n, i: (i, n)),
                   pl.BlockSpec((wo_rows, wo.shape[1]), lambda n, i: (n * n_i + i, 0))],
        out_shape=[jax.ShapeDtypeStruct((t_len, d), _bf16),
                   jax.ShapeDtypeStruct(wo.shape, _bf16)],
        scratch_shapes=[pltpu.VMEM((width, bn), _bf16), pltpu.VMEM((width, bn), _bf16)],
        compiler_params=pltpu.CompilerParams(
            dimension_semantics=("parallel", "arbitrary"), vmem_limit_bytes=VMEM_LIMIT_BYTES),
        name="merge",
    )(ya, yb, gates, gates, gate_bias, wa, wb, wo)


def _out_proj_kernel(m_ref, wo_ref, x_ref, fw_ref, o_ref):
    n = pl.program_id(1)
    bn = x_ref.shape[1]
    for cols in _column_pieces(bn):
        out_cols = pl.ds(pl.multiple_of(n * bn + cols.start, LANES), cols.size)
        o_ref[:, out_cols] = x_ref[:, cols] + _dot(m_ref[...], wo_ref[:, cols])

    @pl.when(n == pl.num_programs(1) - 1)
    def _():
        h = o_ref[...]
        ms = jnp.mean(h * h, axis=-1, keepdims=True)
        o_ref[...] = h * lax.rsqrt(ms + EPS) * fw_ref[...]


def _out_proj(merged, wo, x2d, fw_row, *, bm, bn):
    t_len, d = x2d.shape
    return pl.pallas_call(
        _out_proj_kernel,
        grid=(t_len // bm, d // bn),
        in_specs=[
            pl.BlockSpec((bm, d), lambda i, n: (i, 0)),
            pl.BlockSpec((d, bn), lambda i, n: (0, n)),
            pl.BlockSpec((bm, bn), lambda i, n: (i, n)),
            pl.BlockSpec((1, d), lambda i, n: (0, 0)),
        ],
        out_specs=pl.BlockSpec((bm, d), lambda i, n: (i, 0)),
        out_shape=jax.ShapeDtypeStruct((t_len, d), _f32),
        compiler_params=pltpu.CompilerParams(
            dimension_semantics=("parallel", "arbitrary"), vmem_limit_bytes=VMEM_LIMIT_BYTES),
        name="out_proj",
    )(merged, wo, x2d, fw_row)


def kernel(x, norm_w, w_in, lb_logits, hgrn_norm_w, conv_w, w_branch_a, w_branch_b,
           gate_bias, w_out, final_norm_w):
    bsz, t_len, d = x.shape
    assert bsz == 1 and norm_w.shape[0] == 1, "single sequence, single layer"
    key_w = lb_logits.shape[1]
    val_w = w_branch_a.shape[1]
    conv_wd = conv_w.shape[2]
    assert hgrn_norm_w.shape[1] == LANES and key_w == val_w == conv_wd
    n_heads = key_w // LANES
    sizes = (key_w, key_w, val_w, val_w, conv_wd, conv_wd, conv_wd, conv_wd, d, d)
    offs = [sum(sizes[:i]) for i in range(len(sizes))]
    assert w_in.shape[2] == sum(sizes)

    x2d = x.reshape(t_len, d)
    w2d = w_in.reshape(d, w_in.shape[2])
    u = _prenorm(x2d, norm_w, block_rows=min(512, t_len))
    ya, yb = _proj_mixers(u, w2d, lb_logits, hgrn_norm_w, conv_w[0], n_heads=n_heads,
                          col_offsets=offs[:N_PLANES], bm=min(1024, t_len))
    gates = _in_proj(u, w2d, col0=offs[8], n_cols=2 * d, bm=min(1024, t_len), bn=min(1024, d))
    merged, wo_bf = _merge(ya, yb, gates, gate_bias[0], w_branch_a[0], w_branch_b[0], w_out[0],
                           ga_off=0, gb_off=d, bm=min(1024, t_len), bn=min(512, d))
    out = _out_proj(merged, wo_bf, x2d, final_norm_w.reshape(1, d),
                    bm=min(512, t_len), bn=min(1024, d))
    return out.reshape(bsz, t_len, d)
```

```python
import functools

import jax
import jax.numpy as jnp
from jax import lax
from jax.experimental import pallas as pl
from jax.experimental.pallas import tpu as pltpu

EPS = 1e-6
LANES = 128
SUBLANES = 8
CHUNK = 128
GATE_CHUNKS = 2
DIAG_BLOCK = 16
N_PLANES = 8
PIECE_ROWS = 128
PIECE_COLS = 512
COLUMN_PIECE = 512
MERGE_PIECE_ROWS = 256
CAST_ROWS = 32
VMEM_LIMIT_BYTES = 56 * 1024 * 1024

_f32 = jnp.float32
_bf16 = jnp.bfloat16


def _sigmoid(v):
    return 0.5 * jnp.tanh(0.5 * v) + 0.5


def _silu(v):
    h = 0.5 * v
    return h * jnp.tanh(h) + h


def _dot(a, b):
    return jnp.dot(a, b, preferred_element_type=_f32)


def _dot_nt(a, b):
    return lax.dot_general(a, b, (((1,), (1,)), ((), ())), preferred_element_type=_f32)


def _dot_tn(a, b):
    return lax.dot_general(a, b, (((0,), (0,)), ((), ())), preferred_element_type=_f32)


def _prenorm_kernel(x_ref, w_ref, o_ref):
    x = x_ref[...]
    ms = jnp.mean(x * x, axis=-1, keepdims=True)
    o_ref[...] = (x * lax.rsqrt(ms + EPS) * w_ref[...]).astype(o_ref.dtype)


def _prenorm(x2d, w_row, *, block_rows):
    t_len, d = x2d.shape
    return pl.pallas_call(
        _prenorm_kernel,
        grid=(t_len // block_rows,),
        in_specs=[pl.BlockSpec((block_rows, d), lambda i: (i, 0)),
                  pl.BlockSpec((1, d), lambda i: (0, 0))],
        out_specs=pl.BlockSpec((block_rows, d), lambda i: (i, 0)),
        out_shape=jax.ShapeDtypeStruct((t_len, d), _bf16),
        compiler_params=pltpu.CompilerParams(
            dimension_semantics=("parallel",), vmem_limit_bytes=VMEM_LIMIT_BYTES),
        name="prenorm",
    )(x2d, w_row)


def _row_block_reference(b, pair_rows, ref_row):
    c = b.shape[0]
    grouped = b.reshape(c // pair_rows, pair_rows, LANES)
    picked = grouped[:, ref_row:ref_row + 1, :]
    return jnp.broadcast_to(picked, grouped.shape).reshape(c, LANES)


def _chunk_gates(a_fs, lb, tri):
    ks, splits = [], []
    for a in a_fs:
        f = lb + (1.0 - lb) * _sigmoid(a)
        g = jnp.log2(f)
        g_hi = g.astype(_bf16)
        splits += [g_hi, (g - g_hi.astype(_f32)).astype(_bf16)]
        ks.append(1.0 - f)
    b2 = _dot(tri, jnp.concatenate(splits, axis=1))
    return [(k, b2[:, 2 * i * LANES:2 * (i + 1) * LANES]) for i, k in enumerate(ks)]


def _chunk_scores(q, k, b2):
    c = q.shape[0]
    b = b2[:, :LANES] + b2[:, LANES:]
    ref = _row_block_reference(b, DIAG_BLOCK, DIAG_BLOCK // 2 - 1)
    raw = [_dot_nt((q * jnp.exp2(b - ref)).astype(_bf16), (k * jnp.exp2(ref - b)).astype(_bf16))]
    m = DIAG_BLOCK
    while m < c:
        ref = _row_block_reference(b, 2 * m, m - 1)
        e = jnp.exp2(-jnp.abs(b - ref))
        raw.append(_dot_nt((q * e).astype(_bf16), (k * e).astype(_bf16)))
        m *= 2
    b_last = b[c - 1:c, :]
    qe = (q * jnp.exp2(b)).astype(_bf16)
    ke = (k * jnp.exp2(b_last - b)).astype(_bf16)
    return raw, qe, ke, jnp.exp2(b_last)


def _chunk_output(raw, qe, ke, decay, v, state_t):
    c = qe.shape[0]
    rows = lax.broadcasted_iota(jnp.int32, (c, c), 0)
    cols = lax.broadcasted_iota(jnp.int32, (c, c), 1)
    shift = DIAG_BLOCK.bit_length() - 1
    mask = ((rows >> shift) == (cols >> shift)) & (cols <= rows)
    scores = jnp.where(mask, raw[0], 0.0)
    for lvl, s_lvl in enumerate(raw[1:]):
        rblk = rows >> (shift + lvl)
        mask = ((rblk & 1) == 1) & ((cols >> (shift + lvl)) == rblk - 1)
        scores = jnp.where(mask, s_lvl, scores)
    v_t = v.T.astype(_bf16)
    o = _dot_nt(jnp.concatenate([scores.astype(_bf16), qe], axis=1),
                jnp.concatenate([v_t, state_t.astype(_bf16)], axis=1))
    return o, state_t * decay + _dot(v_t, ke)


def _mixer_stage_groups(t_ref, lbl_ref, hw_ref, cw_ref, ya_ref, yb_ref, state_ref, p_ref,
                        first_block):
    tb = t_ref.shape[0]
    n_chunks = tb // CHUNK

    def plane(idx, rs=slice(None)):
        return t_ref[rs, idx * LANES:(idx + 1) * LANES]

    lbl = lbl_ref[...]
    ex = jnp.exp(lbl - jnp.max(lbl, axis=0, keepdims=True))
    lb = ex[0:1, :] / jnp.sum(ex, axis=0, keepdims=True)
    tri = (lax.broadcasted_iota(jnp.int32, (CHUNK, CHUNK), 1)
           <= lax.broadcasted_iota(jnp.int32, (CHUNK, CHUNK), 0)).astype(_bf16)
    hw = hw_ref[...]

    live = {"state": jnp.where(first_block, 0.0, state_ref[...])}

    def rows_of(c):
        return pl.ds(c * CHUNK, CHUNK)

    def gates(c):
        if c % GATE_CHUNKS == 0:
            cs = range(c, min(c + GATE_CHUNKS, n_chunks))
            for ci, res in zip(cs, _chunk_gates([plane(1, rows_of(ci)) for ci in cs], lb, tri)):
                live[ci] = res

    def scores(c):
        k, b2 = live[c]
        live[c] = _chunk_scores(plane(0, rows_of(c)), k, b2)

    def output(c):
        rs = rows_of(c)
        o, live["state"] = _chunk_output(*live.pop(c), plane(2, rs), live["state"])
        ms = jnp.mean(o * o, axis=-1, keepdims=True)
        gate = plane(3, rs)
        ya = (o * lax.rsqrt(ms + EPS) * hw) * _silu(gate)
        ya_ref[rs, :] = ya.astype(ya_ref.dtype)
        if c == n_chunks - 1:
            state_ref[...] = live["state"]

    groups = []
    for g in range(n_chunks + 2):
        stages = [(output, g - 2), (scores, g - 1), (gates, g)]
        groups.append([functools.partial(fn, c) for fn, c in stages if 0 <= c < n_chunks])

    def conv(c):
        if c == 0:
            p_ref[0:SUBLANES, :] = jnp.where(first_block, 0.0, p_ref[tb:tb + SUBLANES, :])
        rs = rows_of(c)
        p_ref[pl.ds(SUBLANES + c * CHUNK, CHUNK), :] = plane(5, rs) * plane(6, rs)
        k_taps = cw_ref.shape[0]
        acc = None
        for j in range(k_taps):
            off = SUBLANES - (k_taps - 1) + j + c * CHUNK
            term = cw_ref[j:j + 1, :] * p_ref[off:off + CHUNK, :]
            acc = term if acc is None else acc + term
        yb = (plane(4, rs) * acc) * _silu(plane(7, rs))
        yb_ref[rs, :] = yb.astype(yb_ref.dtype)

    return groups, [functools.partial(conv, c) for c in range(n_chunks)]


def _stage_weight_tile(copies_of, tile, n_tiles, stage_ref, wbf_ref):
    for c in copies_of(tile):
        c.wait()

    def cast_rows(r, carry):
        rows = pl.ds(pl.multiple_of(r * CAST_ROWS, CAST_ROWS), CAST_ROWS)
        wbf_ref[rows, :] = stage_ref[rows, :].astype(wbf_ref.dtype)
        return carry

    lax.fori_loop(0, stage_ref.shape[0] // CAST_ROWS, cast_rows, 0)

    @pl.when(tile + 1 < n_tiles)
    def _():
        for c in copies_of(tile + 1):
            c.start()


def _head_weight_copies(w_hbm, stage_ref, sem, head, col_offsets):
    return [pltpu.make_async_copy(
        w_hbm.at[:, pl.ds(pl.multiple_of(off + head * LANES, LANES), LANES)],
        stage_ref.at[:, pl.ds(idx * LANES, LANES)], sem.at[idx])
        for idx, off in enumerate(col_offsets)]


def _proj_mixers_kernel(u_ref, w_hbm, lbl_ref, hw_ref, cw_ref, ya_ref, yb_ref,
                        stage_ref, wbf_ref, t0_ref, t1_ref, state_ref, p_ref, sem,
                        *, n_heads, n_i, col_offsets):
    s = pl.program_id(0)
    n_steps = n_heads * n_i
    head = jnp.minimum(s // n_i, n_heads - 1)
    copies = functools.partial(_head_weight_copies, w_hbm, stage_ref, sem,
                               col_offsets=col_offsets)

    @pl.when(s == 0)
    def _():
        for c in copies(0):
            c.start()
        t1_ref[...] = jnp.zeros_like(t1_ref)
        state_ref[...] = jnp.zeros_like(state_ref)
        p_ref[...] = jnp.zeros_like(p_ref)

    @pl.when((s % n_i == 0) & (s < n_steps))
    def _():
        _stage_weight_tile(copies, head, n_heads, stage_ref, wbf_ref)

    first_block = (jnp.maximum(s - 1, 0) % n_i) == 0
    bm = u_ref.shape[0]
    pieces = [(pl.ds(r, PIECE_ROWS), pl.ds(c, PIECE_COLS))
              for c in range(0, wbf_ref.shape[1], PIECE_COLS) for r in range(0, bm, PIECE_ROWS)]

    def step(t_write, t_read):
        groups, convs = _mixer_stage_groups(t_read, lbl_ref, hw_ref, cw_ref, ya_ref, yb_ref,
                                            state_ref, p_ref, first_block)
        for g, stages in enumerate(groups):
            for stage in stages:
                stage()
            if g < len(convs):
                convs[g]()
            lo, hi = (g * len(pieces)) // len(groups), ((g + 1) * len(pieces)) // len(groups)
            for rows, cols in pieces[lo:hi]:
                t_write[rows, cols] = _dot(u_ref[rows, :], wbf_ref[:, cols])

    @pl.when(s % 2 == 0)
    def _():
        step(t0_ref, t1_ref)

    @pl.when(s % 2 == 1)
    def _():
        step(t1_ref, t0_ref)


def _proj_mixers(u, w, lb_logits, hgrn_w_row, conv_taps, *, n_heads, col_offsets, bm):
    t_len, d = u.shape
    n_i = t_len // bm
    n_steps = n_heads * n_i
    n_layers_p1 = lb_logits.shape[0]
    k_taps = conv_taps.shape[0]
    width = n_heads * LANES
    tile_cols = N_PLANES * LANES

    def prev(s):
        return jnp.maximum(s - 1, 0)

    out_spec = pl.BlockSpec((bm, LANES), lambda s: (prev(s) % n_i, prev(s) // n_i))
    return pl.pallas_call(
        functools.partial(_proj_mixers_kernel, n_heads=n_heads, n_i=n_i,
                          col_offsets=tuple(col_offsets)),
        grid=(n_steps + 1,),
        in_specs=[
            pl.BlockSpec((bm, d), lambda s: (jnp.minimum(s, n_steps - 1) % n_i, 0)),
            pl.BlockSpec(memory_space=pl.ANY),
            pl.BlockSpec((n_layers_p1, LANES), lambda s: (0, prev(s) // n_i)),
            pl.BlockSpec((1, LANES), lambda s: (0, 0)),
            pl.BlockSpec((k_taps, LANES), lambda s: (0, prev(s) // n_i)),
        ],
        out_specs=[out_spec, out_spec],
        out_shape=[jax.ShapeDtypeStruct((t_len, width), _bf16)] * 2,
        scratch_shapes=[
            pltpu.VMEM((d, tile_cols), _f32),
            pltpu.VMEM((d, tile_cols), _bf16),
            pltpu.VMEM((bm, tile_cols), _f32),
            pltpu.VMEM((bm, tile_cols), _f32),
            pltpu.VMEM((LANES, LANES), _f32),
            pltpu.VMEM((bm + SUBLANES, LANES), _f32),
            pltpu.SemaphoreType.DMA((N_PLANES,)),
        ],
        compiler_params=pltpu.CompilerParams(
            dimension_semantics=("arbitrary",), vmem_limit_bytes=VMEM_LIMIT_BYTES),
        name="proj_mixers",
    )(u, w, lb_logits, hgrn_w_row, conv_taps)


def _column_pieces(n_cols):
    width = min(COLUMN_PIECE, n_cols)
    return [pl.ds(c, width) for c in range(0, n_cols, width)]


def _in_proj_kernel(u_ref, w_hbm, wo_ref, o_ref, wo_bf_ref, stage_ref, wbf_ref, sem,
                    *, n_tiles, n_i, col0):
    s = pl.program_id(0)
    wo_bf_ref[...] = wo_ref[...].astype(wo_bf_ref.dtype)
    bn = wbf_ref.shape[1]

    def copies(tile):
        cols = pl.ds(pl.multiple_of(col0 + tile * bn, LANES), bn)
        return [pltpu.make_async_copy(w_hbm.at[:, cols], stage_ref, sem.at[0])]

    @pl.when(s == 0)
    def _():
        for c in copies(0):
            c.start()

    @pl.when(s % n_i == 0)
    def _():
        _stage_weight_tile(copies, s // n_i, n_tiles, stage_ref, wbf_ref)

    for cols in _column_pieces(bn):
        o_ref[:, cols] = _dot(u_ref[...], wbf_ref[:, cols]).astype(o_ref.dtype)


def _in_proj(u, w, wo, *, col0, n_cols, bm, bn):
    t_len, d = u.shape
    n_i, n_tiles = t_len // bm, n_cols // bn
    n_steps = n_tiles * n_i
    wo_rows = wo.shape[0] // n_steps
    assert wo_rows * n_steps == wo.shape[0] and wo_rows % 16 == 0
    wo_spec = pl.BlockSpec((wo_rows, wo.shape[1]), lambda s: (s, 0))
    return pl.pallas_call(
        functools.partial(_in_proj_kernel, n_tiles=n_tiles, n_i=n_i, col0=col0),
        grid=(n_steps,),
        in_specs=[pl.BlockSpec((bm, d), lambda s: (s % n_i, 0)),
                  pl.BlockSpec(memory_space=pl.ANY),
                  wo_spec],
        out_specs=[pl.BlockSpec((bm, bn), lambda s: (s % n_i, s // n_i)), wo_spec],
        out_shape=[jax.ShapeDtypeStruct((t_len, n_cols), _bf16),
                   jax.ShapeDtypeStruct(wo.shape, _bf16)],
        scratch_shapes=[pltpu.VMEM((d, bn), _f32), pltpu.VMEM((d, bn), _bf16),
                        pltpu.SemaphoreType.DMA((1,))],
        compiler_params=pltpu.CompilerParams(
            dimension_semantics=("arbitrary",), vmem_limit_bytes=VMEM_LIMIT_BYTES),
        name="in_proj",
    )(u, w, wo)


def _merge_kernel(ya_ref, yb_ref, ga_ref, gb_ref, bias_ref, wa_hbm, wb_hbm, o_ref,
                  stage_ref, wbf_ref, sem, *, n_tiles, n_i):
    s = pl.program_id(0)
    width = wa_hbm.shape[0]
    bn = o_ref.shape[1]

    def copies(tile):
        cols = pl.ds(pl.multiple_of(tile * bn, LANES), bn)
        return [pltpu.make_async_copy(wa_hbm.at[:, cols], stage_ref.at[0:width, :], sem.at[0]),
                pltpu.make_async_copy(wb_hbm.at[:, cols], stage_ref.at[width:2 * width, :],
                                      sem.at[1])]

    @pl.when(s == 0)
    def _():
        for c in copies(0):
            c.start()

    @pl.when(s % n_i == 0)
    def _():
        _stage_weight_tile(copies, s // n_i, n_tiles, stage_ref, wbf_ref)

    for cols in _column_pieces(bn):
        for r in range(0, o_ref.shape[0], MERGE_PIECE_ROWS):
            rows = pl.ds(r, MERGE_PIECE_ROWS)
            za = _dot(ya_ref[rows, :], wbf_ref[0:width, cols])
            zb = _dot(yb_ref[rows, :], wbf_ref[width:2 * width, cols])
            gate_a = _sigmoid(ga_ref[rows, cols].astype(_f32) + bias_ref[0:1, cols])
            gate_b = _sigmoid(gb_ref[rows, cols].astype(_f32) + bias_ref[1:2, cols])
            o_ref[rows, cols] = (gate_a * za + gate_b * zb).astype(o_ref.dtype)


def _merge(ya, yb, gates, gate_bias, wa, wb, *, ga_off, gb_off, bm, bn):
    t_len, width = ya.shape
    d = wa.shape[1]
    ga_base, gb_base = ga_off // bn, gb_off // bn
    n_i, n_tiles = t_len // bm, d // bn
    return pl.pallas_call(
        functools.partial(_merge_kernel, n_tiles=n_tiles, n_i=n_i),
        grid=(n_tiles * n_i,),
        in_specs=[
            pl.BlockSpec((bm, width), lambda s: (s % n_i, 0)),
            pl.BlockSpec((bm, width), lambda s: (s % n_i, 0)),
            pl.BlockSpec((bm, bn), lambda s: (s % n_i, ga_base + s // n_i)),
            pl.BlockSpec((bm, bn), lambda s: (s % n_i, gb_base + s // n_i)),
            pl.BlockSpec((2, bn), lambda s: (0, s // n_i)),
            pl.BlockSpec(memory_space=pl.ANY),
            pl.BlockSpec(memory_space=pl.ANY),
        ],
        out_specs=pl.BlockSpec((bm, bn), lambda s: (s % n_i, s // n_i)),
        out_shape=jax.ShapeDtypeStruct((t_len, d), _bf16),
        scratch_shapes=[pltpu.VMEM((2 * width, bn), _f32), pltpu.VMEM((2 * width, bn), _bf16),
                        pltpu.SemaphoreType.DMA((2,))],
        compiler_params=pltpu.CompilerParams(
            dimension_semantics=("arbitrary",), vmem_limit_bytes=VMEM_LIMIT_BYTES),
        name="merge",
    )(ya, yb, gates, gates, gate_bias, wa, wb)


def _out_proj_kernel(m_ref, wo_ref, x_ref, fw_ref, o_ref):
    for cols in _column_pieces(o_ref.shape[1]):
        o_ref[:, cols] = x_ref[:, cols] + _dot(m_ref[...], wo_ref[:, cols])
    h = o_ref[...]
    ms = jnp.mean(h * h, axis=-1, keepdims=True)
    o_ref[...] = h * lax.rsqrt(ms + EPS) * fw_ref[...]


def _out_proj(merged, wo, x2d, fw_row, *, bm):
    t_len, d = x2d.shape
    return pl.pallas_call(
        _out_proj_kernel,
        grid=(t_len // bm,),
        in_specs=[
            pl.BlockSpec((bm, d), lambda i: (i, 0)),
            pl.BlockSpec((d, d), lambda i: (0, 0), pipeline_mode=pl.Buffered(1)),
            pl.BlockSpec((bm, d), lambda i: (i, 0)),
            pl.BlockSpec((1, d), lambda i: (0, 0)),
        ],
        out_specs=pl.BlockSpec((bm, d), lambda i: (i, 0)),
        out_shape=jax.ShapeDtypeStruct((t_len, d), _f32),
        compiler_params=pltpu.CompilerParams(
            dimension_semantics=("parallel",), vmem_limit_bytes=VMEM_LIMIT_BYTES),
        name="out_proj",
    )(merged, wo, x2d, fw_row)


def kernel(x, norm_w, w_in, lb_logits, hgrn_norm_w, conv_w, w_branch_a, w_branch_b,
           gate_bias, w_out, final_norm_w):
    bsz, t_len, d = x.shape
    assert bsz == 1 and norm_w.shape[0] == 1, "single sequence, single layer"
    key_w = lb_logits.shape[1]
    val_w = w_branch_a.shape[1]
    conv_wd = conv_w.shape[2]
    assert hgrn_norm_w.shape[1] == LANES and key_w == val_w == conv_wd
    n_heads = key_w // LANES
    sizes = (key_w, key_w, val_w, val_w, conv_wd, conv_wd, conv_wd, conv_wd, d, d)
    offs = [sum(sizes[:i]) for i in range(len(sizes))]
    assert w_in.shape[2] == sum(sizes)

    x2d = x.reshape(t_len, d)
    w2d = w_in.reshape(d, w_in.shape[2])
    u = _prenorm(x2d, norm_w, block_rows=min(512, t_len))
    ya, yb = _proj_mixers(u, w2d, lb_logits, hgrn_norm_w, conv_w[0], n_heads=n_heads,
                          col_offsets=offs[:N_PLANES], bm=min(1024, t_len))
    gates, wo_bf = _in_proj(u, w2d, w_out[0], col0=offs[8], n_cols=2 * d,
                            bm=min(1024, t_len), bn=min(1024, d))
    merged = _merge(ya, yb, gates, gate_bias[0], w_branch_a[0], w_branch_b[0],
                    ga_off=0, gb_off=d, bm=min(512, t_len), bn=min(1024, d))
    out = _out_proj(merged, wo_bf, x2d, final_norm_w.reshape(1, d), bm=min(128, t_len))
    return out.reshape(bsz, t_len, d)
```

```python
import functools

import jax
import jax.numpy as jnp
from jax import lax
from jax.experimental import pallas as pl
from jax.experimental.pallas import tpu as pltpu

EPS = 1e-6
LANES = 128
SUBLANES = 8
CHUNK = 128
GATE_CHUNKS = 2
DIAG_BLOCK = 16
N_PLANES = 8
PIECE_ROWS = 128
PIECE_COLS = 512
COLUMN_PIECE = 512
MERGE_PIECE_ROWS = 256
CAST_ROWS = 32
VMEM_LIMIT_BYTES = 56 * 1024 * 1024
OUT_PROJ_VMEM_LIMIT_BYTES = 60 * 1024 * 1024

_f32 = jnp.float32
_bf16 = jnp.bfloat16


def _sigmoid(v):
    return 0.5 * jnp.tanh(0.5 * v) + 0.5


def _silu(v):
    h = 0.5 * v
    return h * jnp.tanh(h) + h


def _dot(a, b):
    return jnp.dot(a, b, preferred_element_type=_f32)


def _dot_nt(a, b):
    return lax.dot_general(a, b, (((1,), (1,)), ((), ())), preferred_element_type=_f32)


def _prenorm_kernel(x_ref, w_ref, o_ref):
    x = x_ref[...]
    ms = jnp.mean(x * x, axis=-1, keepdims=True)
    o_ref[...] = (x * lax.rsqrt(ms + EPS) * w_ref[...]).astype(o_ref.dtype)


def _prenorm(x2d, w_row, *, block_rows):
    t_len, d = x2d.shape
    return pl.pallas_call(
        _prenorm_kernel,
        grid=(t_len // block_rows,),
        in_specs=[pl.BlockSpec((block_rows, d), lambda i: (i, 0)),
                  pl.BlockSpec((1, d), lambda i: (0, 0))],
        out_specs=pl.BlockSpec((block_rows, d), lambda i: (i, 0)),
        out_shape=jax.ShapeDtypeStruct((t_len, d), _bf16),
        compiler_params=pltpu.CompilerParams(
            dimension_semantics=("parallel",), vmem_limit_bytes=VMEM_LIMIT_BYTES),
        name="prenorm",
    )(x2d, w_row)


def _row_block_reference(b, pair_rows, ref_row):
    c = b.shape[0]
    grouped = b.reshape(c // pair_rows, pair_rows, LANES)
    picked = grouped[:, ref_row:ref_row + 1, :]
    return jnp.broadcast_to(picked, grouped.shape).reshape(c, LANES)


def _chunk_gates(a_fs, lb, tri):
    ks, splits = [], []
    for a in a_fs:
        f = lb + (1.0 - lb) * _sigmoid(a)
        g = jnp.log2(f)
        g_hi = g.astype(_bf16)
        splits += [g_hi, (g - g_hi.astype(_f32)).astype(_bf16)]
        ks.append(1.0 - f)
    b2 = _dot(tri, jnp.concatenate(splits, axis=1))
    return [(k, b2[:, 2 * i * LANES:2 * (i + 1) * LANES]) for i, k in enumerate(ks)]


def _chunk_scores(q, k, b2):
    c = q.shape[0]
    b = b2[:, :LANES] + b2[:, LANES:]
    ref = _row_block_reference(b, DIAG_BLOCK, DIAG_BLOCK // 2 - 1)
    raw = [_dot((q * jnp.exp2(b - ref)).astype(_bf16), (k * jnp.exp2(ref - b)).T.astype(_bf16))]
    m = DIAG_BLOCK
    while m < c:
        ref = _row_block_reference(b, 2 * m, m - 1)
        e = jnp.exp2(-jnp.abs(b - ref))
        raw.append(_dot((q * e).astype(_bf16), (k * e).T.astype(_bf16)))
        m *= 2
    b_last = b[c - 1:c, :]
    qe = (q * jnp.exp2(b)).astype(_bf16)
    ke = (k * jnp.exp2(b_last - b)).astype(_bf16)
    return raw, qe, ke, jnp.exp2(b_last)


def _chunk_output(raw, qe, ke, decay, v, state_t):
    c = qe.shape[0]
    rows = lax.broadcasted_iota(jnp.int32, (c, c), 0)
    cols = lax.broadcasted_iota(jnp.int32, (c, c), 1)
    shift = DIAG_BLOCK.bit_length() - 1
    mask = ((rows >> shift) == (cols >> shift)) & (cols <= rows)
    scores = jnp.where(mask, raw[0], 0.0)
    for lvl, s_lvl in enumerate(raw[1:]):
        rblk = rows >> (shift + lvl)
        mask = ((rblk & 1) == 1) & ((cols >> (shift + lvl)) == rblk - 1)
        scores = jnp.where(mask, s_lvl, scores)
    v_t = v.T.astype(_bf16)
    o = _dot_nt(jnp.concatenate([scores.astype(_bf16), qe], axis=1),
                jnp.concatenate([v_t, state_t.astype(_bf16)], axis=1))
    return o, state_t * decay + _dot(v_t, ke)


def _mixer_stage_groups(t_ref, lbl_ref, hw_ref, cw_ref, ya_ref, yb_ref, state_ref, p_ref,
                        first_block):
    tb = t_ref.shape[0]
    n_chunks = tb // CHUNK

    def plane(idx, rs=slice(None)):
        return t_ref[rs, idx * LANES:(idx + 1) * LANES]

    lbl = lbl_ref[...]
    ex = jnp.exp(lbl - jnp.max(lbl, axis=0, keepdims=True))
    lb = ex[0:1, :] / jnp.sum(ex, axis=0, keepdims=True)
    tri = (lax.broadcasted_iota(jnp.int32, (CHUNK, CHUNK), 1)
           <= lax.broadcasted_iota(jnp.int32, (CHUNK, CHUNK), 0)).astype(_bf16)
    hw = hw_ref[...]

    live = {"state": jnp.where(first_block, 0.0, state_ref[...])}

    def rows_of(c):
        return pl.ds(c * CHUNK, CHUNK)

    def gates(c):
        if c % GATE_CHUNKS == 0:
            cs = range(c, min(c + GATE_CHUNKS, n_chunks))
            for ci, res in zip(cs, _chunk_gates([plane(1, rows_of(ci)) for ci in cs], lb, tri)):
                live[ci] = res

    def scores(c):
        k, b2 = live[c]
        live[c] = _chunk_scores(plane(0, rows_of(c)), k, b2)

    def output(c):
        rs = rows_of(c)
        o, live["state"] = _chunk_output(*live.pop(c), plane(2, rs), live["state"])
        ms = jnp.mean(o * o, axis=-1, keepdims=True)
        gate = plane(3, rs)
        ya = (o * lax.rsqrt(ms + EPS) * hw) * _silu(gate)
        ya_ref[rs, :] = ya.astype(ya_ref.dtype)
        if c == n_chunks - 1:
            state_ref[...] = live["state"]

    groups = []
    for g in range(n_chunks + 2):
        stages = [(output, g - 2), (scores, g - 1), (gates, g)]
        groups.append([functools.partial(fn, c) for fn, c in stages if 0 <= c < n_chunks])

    def conv(c):
        if c == 0:
            p_ref[0:SUBLANES, :] = jnp.where(first_block, 0.0, p_ref[tb:tb + SUBLANES, :])
        rs = rows_of(c)
        p_ref[pl.ds(SUBLANES + c * CHUNK, CHUNK), :] = plane(5, rs) * plane(6, rs)
        k_taps = cw_ref.shape[0]
        acc = None
        for j in range(k_taps):
            off = SUBLANES - (k_taps - 1) + j + c * CHUNK
            term = cw_ref[j:j + 1, :] * p_ref[off:off + CHUNK, :]
            acc = term if acc is None else acc + term
        yb = (plane(4, rs) * acc) * _silu(plane(7, rs))
        yb_ref[rs, :] = yb.astype(yb_ref.dtype)

    return groups, [functools.partial(conv, c) for c in range(n_chunks)]


def _stage_weight_tile(copies_of, tile, n_tiles, stage_ref, wbf_ref):
    for c in copies_of(tile):
        c.wait()

    def cast_rows(r, carry):
        rows = pl.ds(pl.multiple_of(r * CAST_ROWS, CAST_ROWS), CAST_ROWS)
        wbf_ref[rows, :] = stage_ref[rows, :].astype(wbf_ref.dtype)
        return carry

    lax.fori_loop(0, stage_ref.shape[0] // CAST_ROWS, cast_rows, 0)

    @pl.when(tile + 1 < n_tiles)
    def _():
        for c in copies_of(tile + 1):
            c.start()


def _head_weight_copies(w_hbm, stage_ref, sem, head, col_offsets):
    return [pltpu.make_async_copy(
        w_hbm.at[:, pl.ds(pl.multiple_of(off + head * LANES, LANES), LANES)],
        stage_ref.at[:, pl.ds(idx * LANES, LANES)], sem.at[idx])
        for idx, off in enumerate(col_offsets)]


def _proj_mixers_kernel(u_ref, w_hbm, lbl_ref, hw_ref, cw_ref, ya_ref, yb_ref,
                        stage_ref, wbf_ref, t0_ref, t1_ref, state_ref, p_ref, sem,
                        *, n_heads, n_i, col_offsets):
    s = pl.program_id(0)
    n_steps = n_heads * n_i
    head = jnp.minimum(s // n_i, n_heads - 1)
    copies = functools.partial(_head_weight_copies, w_hbm, stage_ref, sem,
                               col_offsets=col_offsets)

    @pl.when(s == 0)
    def _():
        for c in copies(0):
            c.start()
        t1_ref[...] = jnp.zeros_like(t1_ref)
        state_ref[...] = jnp.zeros_like(state_ref)
        p_ref[...] = jnp.zeros_like(p_ref)

    @pl.when((s % n_i == 0) & (s < n_steps))
    def _():
        _stage_weight_tile(copies, head, n_heads, stage_ref, wbf_ref)

    first_block = (jnp.maximum(s - 1, 0) % n_i) == 0
    bm = u_ref.shape[0]
    pieces = [(pl.ds(r, PIECE_ROWS), pl.ds(c, PIECE_COLS))
              for c in range(0, wbf_ref.shape[1], PIECE_COLS) for r in range(0, bm, PIECE_ROWS)]

    def step(t_write, t_read):
        groups, convs = _mixer_stage_groups(t_read, lbl_ref, hw_ref, cw_ref, ya_ref, yb_ref,
                                            state_ref, p_ref, first_block)
        for g, stages in enumerate(groups):
            for stage in stages:
                stage()
            if g < len(convs):
                convs[g]()
            lo, hi = (g * len(pieces)) // len(groups), ((g + 1) * len(pieces)) // len(groups)
            for rows, cols in pieces[lo:hi]:
                t_write[rows, cols] = _dot(u_ref[rows, :], wbf_ref[:, cols])

    @pl.when(s % 2 == 0)
    def _():
        step(t0_ref, t1_ref)

    @pl.when(s % 2 == 1)
    def _():
        step(t1_ref, t0_ref)


def _proj_mixers(u, w, lb_logits, hgrn_w_row, conv_taps, *, n_heads, col_offsets, bm):
    t_len, d = u.shape
    n_i = t_len // bm
    n_steps = n_heads * n_i
    n_layers_p1 = lb_logits.shape[0]
    k_taps = conv_taps.shape[0]
    width = n_heads * LANES
    tile_cols = N_PLANES * LANES

    def prev(s):
        return jnp.maximum(s - 1, 0)

    out_spec = pl.BlockSpec((bm, LANES), lambda s: (prev(s) % n_i, prev(s) // n_i))
    return pl.pallas_call(
        functools.partial(_proj_mixers_kernel, n_heads=n_heads, n_i=n_i,
                          col_offsets=tuple(col_offsets)),
        grid=(n_steps + 1,),
        in_specs=[
            pl.BlockSpec((bm, d), lambda s: (jnp.minimum(s, n_steps - 1) % n_i, 0)),
            pl.BlockSpec(memory_space=pl.ANY),
            pl.BlockSpec((n_layers_p1, LANES), lambda s: (0, prev(s) // n_i)),
            pl.BlockSpec((1, LANES), lambda s: (0, 0)),
            pl.BlockSpec((k_taps, LANES), lambda s: (0, prev(s) // n_i)),
        ],
        out_specs=[out_spec, out_spec],
        out_shape=[jax.ShapeDtypeStruct((t_len, width), _bf16)] * 2,
        scratch_shapes=[
            pltpu.VMEM((d, tile_cols), _f32),
            pltpu.VMEM((d, tile_cols), _bf16),
            pltpu.VMEM((bm, tile_cols), _f32),
            pltpu.VMEM((bm, tile_cols), _f32),
            pltpu.VMEM((LANES, LANES), _f32),
            pltpu.VMEM((bm + SUBLANES, LANES), _f32),
            pltpu.SemaphoreType.DMA((N_PLANES,)),
        ],
        compiler_params=pltpu.CompilerParams(
            dimension_semantics=("arbitrary",), vmem_limit_bytes=VMEM_LIMIT_BYTES),
        name="proj_mixers",
    )(u, w, lb_logits, hgrn_w_row, conv_taps)


def _column_pieces(n_cols):
    width = min(COLUMN_PIECE, n_cols)
    return [pl.ds(c, width) for c in range(0, n_cols, width)]


def _in_proj_kernel(u_ref, w_hbm, wo_ref, o_ref, wo_bf_ref, stage_ref, wbf_ref, sem,
                    *, n_tiles, n_i, col0):
    s = pl.program_id(0)
    wo_bf_ref[...] = wo_ref[...].astype(wo_bf_ref.dtype)
    bn = wbf_ref.shape[1]

    def copies(tile):
        cols = pl.ds(pl.multiple_of(col0 + tile * bn, LANES), bn)
        return [pltpu.make_async_copy(w_hbm.at[:, cols], stage_ref, sem.at[0])]

    @pl.when(s == 0)
    def _():
        for c in copies(0):
            c.start()

    @pl.when(s % n_i == 0)
    def _():
        _stage_weight_tile(copies, s // n_i, n_tiles, stage_ref, wbf_ref)

    for cols in _column_pieces(bn):
        o_ref[:, cols] = _dot(u_ref[...], wbf_ref[:, cols]).astype(o_ref.dtype)


def _in_proj(u, w, wo, *, col0, n_cols, bm, bn):
    t_len, d = u.shape
    n_i, n_tiles = t_len // bm, n_cols // bn
    n_steps = n_tiles * n_i
    wo_rows = wo.shape[0] // n_steps
    assert wo_rows * n_steps == wo.shape[0] and wo_rows % 16 == 0
    wo_spec = pl.BlockSpec((wo_rows, wo.shape[1]), lambda s: (s, 0))
    return pl.pallas_call(
        functools.partial(_in_proj_kernel, n_tiles=n_tiles, n_i=n_i, col0=col0),
        grid=(n_steps,),
        in_specs=[pl.BlockSpec((bm, d), lambda s: (s % n_i, 0)),
                  pl.BlockSpec(memory_space=pl.ANY),
                  wo_spec],
        out_specs=[pl.BlockSpec((bm, bn), lambda s: (s % n_i, s // n_i)), wo_spec],
        out_shape=[jax.ShapeDtypeStruct((t_len, n_cols), _bf16),
                   jax.ShapeDtypeStruct(wo.shape, _bf16)],
        scratch_shapes=[pltpu.VMEM((d, bn), _f32), pltpu.VMEM((d, bn), _bf16),
                        pltpu.SemaphoreType.DMA((1,))],
        compiler_params=pltpu.CompilerParams(
            dimension_semantics=("arbitrary",), vmem_limit_bytes=VMEM_LIMIT_BYTES),
        name="in_proj",
    )(u, w, wo)


def _merge_kernel(ya_ref, yb_ref, ga_ref, gb_ref, bias_ref, wa_hbm, wb_hbm, o_ref,
                  stage_ref, wbf_ref, sem, *, n_tiles, n_i):
    s = pl.program_id(0)
    width = wa_hbm.shape[0]
    bn = o_ref.shape[1]

    def copies(tile):
        cols = pl.ds(pl.multiple_of(tile * bn, LANES), bn)
        return [pltpu.make_async_copy(wa_hbm.at[:, cols], stage_ref.at[0:width, :], sem.at[0]),
                pltpu.make_async_copy(wb_hbm.at[:, cols], stage_ref.at[width:2 * width, :],
                                      sem.at[1])]

    @pl.when(s == 0)
    def _():
        for c in copies(0):
            c.start()

    @pl.when(s % n_i == 0)
    def _():
        _stage_weight_tile(copies, s // n_i, n_tiles, stage_ref, wbf_ref)

    for cols in _column_pieces(bn):
        for r in range(0, o_ref.shape[0], MERGE_PIECE_ROWS):
            rows = pl.ds(r, MERGE_PIECE_ROWS)
            za = _dot(ya_ref[rows, :], wbf_ref[0:width, cols])
            zb = _dot(yb_ref[rows, :], wbf_ref[width:2 * width, cols])
            gate_a = _sigmoid(ga_ref[rows, cols].astype(_f32) + bias_ref[0:1, cols])
            gate_b = _sigmoid(gb_ref[rows, cols].astype(_f32) + bias_ref[1:2, cols])
            o_ref[rows, cols] = (gate_a * za + gate_b * zb).astype(o_ref.dtype)


def _merge(ya, yb, gates, gate_bias, wa, wb, *, ga_off, gb_off, bm, bn):
    t_len, width = ya.shape
    d = wa.shape[1]
    ga_base, gb_base = ga_off // bn, gb_off // bn
    n_i, n_tiles = t_len // bm, d // bn
    return pl.pallas_call(
        functools.partial(_merge_kernel, n_tiles=n_tiles, n_i=n_i),
        grid=(n_tiles * n_i,),
        in_specs=[
            pl.BlockSpec((bm, width), lambda s: (s % n_i, 0)),
            pl.BlockSpec((bm, width), lambda s: (s % n_i, 0)),
            pl.BlockSpec((bm, bn), lambda s: (s % n_i, ga_base + s // n_i)),
            pl.BlockSpec((bm, bn), lambda s: (s % n_i, gb_base + s // n_i)),
            pl.BlockSpec((2, bn), lambda s: (0, s // n_i)),
            pl.BlockSpec(memory_space=pl.ANY),
            pl.BlockSpec(memory_space=pl.ANY),
        ],
        out_specs=pl.BlockSpec((bm, bn), lambda s: (s % n_i, s // n_i)),
        out_shape=jax.ShapeDtypeStruct((t_len, d), _bf16),
        scratch_shapes=[pltpu.VMEM((2 * width, bn), _f32), pltpu.VMEM((2 * width, bn), _bf16),
                        pltpu.SemaphoreType.DMA((2,))],
        compiler_params=pltpu.CompilerParams(
            dimension_semantics=("arbitrary",), vmem_limit_bytes=VMEM_LIMIT_BYTES),
        name="merge",
    )(ya, yb, gates, gates, gate_bias, wa, wb)


def _out_proj_kernel(m_ref, wo_ref, x_ref, fw_ref, o_ref):
    for cols in _column_pieces(o_ref.shape[1]):
        o_ref[:, cols] = x_ref[:, cols] + _dot(m_ref[...], wo_ref[:, cols])
    h = o_ref[...]
    ms = jnp.mean(h * h, axis=-1, keepdims=True)
    o_ref[...] = h * lax.rsqrt(ms + EPS) * fw_ref[...]


def _out_proj(merged, wo, x2d, fw_row, *, bm):
    t_len, d = x2d.shape
    return pl.pallas_call(
        _out_proj_kernel,
        grid=(t_len // bm,),
        in_specs=[
            pl.BlockSpec((bm, d), lambda i: (i, 0)),
            pl.BlockSpec((d, d), lambda i: (0, 0), pipeline_mode=pl.Buffered(1)),
            pl.BlockSpec((bm, d), lambda i: (i, 0)),
            pl.BlockSpec((1, d), lambda i: (0, 0)),
        ],
        out_specs=pl.BlockSpec((bm, d), lambda i: (i, 0)),
        out_shape=jax.ShapeDtypeStruct((t_len, d), _f32),
        compiler_params=pltpu.CompilerParams(
            dimension_semantics=("parallel",), vmem_limit_bytes=OUT_PROJ_VMEM_LIMIT_BYTES),
        name="out_proj",
    )(merged, wo, x2d, fw_row)


def _block_sizes(t_len, d):
    return dict(prenorm_rows=min(512, t_len), mixers_rows=min(1024, t_len),
                gates_rows=min(1024, t_len), gates_cols=min(1024, d),
                merge_rows=min(512, t_len), merge_cols=min(1024, d), out_rows=min(256, t_len))


def kernel(x, norm_w, w_in, lb_logits, hgrn_norm_w, conv_w, w_branch_a, w_branch_b,
           gate_bias, w_out, final_norm_w):
    bsz, t_len, d = x.shape
    assert bsz == 1 and norm_w.shape[0] == 1, "single sequence, single layer"
    key_w = lb_logits.shape[1]
    val_w = w_branch_a.shape[1]
    conv_wd = conv_w.shape[2]
    assert hgrn_norm_w.shape[1] == LANES and key_w == val_w == conv_wd
    n_heads = key_w // LANES
    sizes = (key_w, key_w, val_w, val_w, conv_wd, conv_wd, conv_wd, conv_wd, d, d)
    offs = [sum(sizes[:i]) for i in range(len(sizes))]
    assert w_in.shape[2] == sum(sizes)

    x2d = x.reshape(t_len, d)
    w2d = w_in.reshape(d, w_in.shape[2])
    blk = _block_sizes(t_len, d)
    u = _prenorm(x2d, norm_w, block_rows=blk["prenorm_rows"])
    ya, yb = _proj_mixers(u, w2d, lb_logits, hgrn_norm_w, conv_w[0], n_heads=n_heads,
                          col_offsets=offs[:N_PLANES], bm=blk["mixers_rows"])
    gates, wo_bf = _in_proj(u, w2d, w_out[0], col0=offs[N_PLANES], n_cols=2 * d,
                            bm=blk["gates_rows"], bn=blk["gates_cols"])
    merged = _merge(ya, yb, gates, gate_bias[0], w_branch_a[0], w_branch_b[0],
                    ga_off=0, gb_off=d, bm=blk["merge_rows"], bn=blk["merge_cols"])
    out = _out_proj(merged, wo_bf, x2d, final_norm_w.reshape(1, d), bm=blk["out_rows"])
    return out.reshape(bsz, t_len, d)
```

```python
import functools

import jax
import jax.numpy as jnp
from jax import lax
from jax.experimental import pallas as pl
from jax.experimental.pallas import tpu as pltpu

EPS = 1e-6
LANES = 128
SUBLANES = 8
CHUNK = 128
GATE_CHUNKS = 2
DIAG_BLOCK = 16
N_PLANES = 8
PIECE_ROWS = 128
PIECE_COLS = 512
COLUMN_PIECE = 512
MERGE_PIECE_ROWS = 256
CAST_ROWS = 32
VMEM_LIMIT_BYTES = 56 * 1024 * 1024
OUT_PROJ_VMEM_LIMIT_BYTES = 60 * 1024 * 1024

_f32 = jnp.float32
_bf16 = jnp.bfloat16


def _sigmoid(v):
    return 0.5 * jnp.tanh(0.5 * v) + 0.5


def _silu(v):
    h = 0.5 * v
    return h * jnp.tanh(h) + h


def _dot(a, b):
    return jnp.dot(a, b, preferred_element_type=_f32)


def _dot_nt(a, b):
    return lax.dot_general(a, b, (((1,), (1,)), ((), ())), preferred_element_type=_f32)


def _prenorm_kernel(x_ref, w_ref, o_ref):
    x = x_ref[...]
    ms = jnp.mean(x * x, axis=-1, keepdims=True)
    o_ref[...] = (x * lax.rsqrt(ms + EPS) * w_ref[...]).astype(o_ref.dtype)


def _prenorm(x2d, w_row, *, block_rows):
    t_len, d = x2d.shape
    return pl.pallas_call(
        _prenorm_kernel,
        grid=(t_len // block_rows,),
        in_specs=[pl.BlockSpec((block_rows, d), lambda i: (i, 0)),
                  pl.BlockSpec((1, d), lambda i: (0, 0))],
        out_specs=pl.BlockSpec((block_rows, d), lambda i: (i, 0)),
        out_shape=jax.ShapeDtypeStruct((t_len, d), _bf16),
        compiler_params=pltpu.CompilerParams(
            dimension_semantics=("parallel",), vmem_limit_bytes=VMEM_LIMIT_BYTES),
        name="prenorm",
    )(x2d, w_row)


def _row_block_reference(b, pair_rows, ref_row):
    c = b.shape[0]
    grouped = b.reshape(c // pair_rows, pair_rows, LANES)
    picked = grouped[:, ref_row:ref_row + 1, :]
    return jnp.broadcast_to(picked, grouped.shape).reshape(c, LANES)


def _chunk_gates(a_fs, lb, tri):
    ks, splits = [], []
    for a in a_fs:
        f = lb + (1.0 - lb) * _sigmoid(a)
        g = jnp.log2(f)
        g_hi = g.astype(_bf16)
        splits += [g_hi, (g - g_hi.astype(_f32)).astype(_bf16)]
        ks.append(1.0 - f)
    b2 = _dot(tri, jnp.concatenate(splits, axis=1))
    return [(k, b2[:, 2 * i * LANES:2 * (i + 1) * LANES]) for i, k in enumerate(ks)]


def _chunk_scores(q, k, b2):
    c = q.shape[0]
    b = b2[:, :LANES] + b2[:, LANES:]
    ref = _row_block_reference(b, DIAG_BLOCK, DIAG_BLOCK // 2 - 1)
    raw = [_dot((q * jnp.exp2(b - ref)).astype(_bf16), (k * jnp.exp2(ref - b)).T.astype(_bf16))]
    m = DIAG_BLOCK
    while m < c:
        ref = _row_block_reference(b, 2 * m, m - 1)
        e = jnp.exp2(-jnp.abs(b - ref))
        raw.append(_dot((q * e).astype(_bf16), (k * e).T.astype(_bf16)))
        m *= 2
    b_last = b[c - 1:c, :]
    qe = (q * jnp.exp2(b)).astype(_bf16)
    ke = (k * jnp.exp2(b_last - b)).astype(_bf16)
    return raw, qe, ke, jnp.exp2(b_last)


def _chunk_output(raw, qe, ke, decay, v, state_t):
    c = qe.shape[0]
    rows = lax.broadcasted_iota(jnp.int32, (c, c), 0)
    cols = lax.broadcasted_iota(jnp.int32, (c, c), 1)
    shift = DIAG_BLOCK.bit_length() - 1
    mask = ((rows >> shift) == (cols >> shift)) & (cols <= rows)
    scores = jnp.where(mask, raw[0], 0.0)
    for lvl, s_lvl in enumerate(raw[1:]):
        rblk = rows >> (shift + lvl)
        mask = ((rblk & 1) == 1) & ((cols >> (shift + lvl)) == rblk - 1)
        scores = jnp.where(mask, s_lvl, scores)
    v_t = v.T.astype(_bf16)
    o = _dot_nt(jnp.concatenate([scores.astype(_bf16), qe], axis=1),
                jnp.concatenate([v_t, state_t.astype(_bf16)], axis=1))
    return o, state_t * decay + _dot(v_t, ke)


def _mixer_stage_groups(t_ref, lbl_ref, hw_ref, cw_ref, ya_ref, yb_ref, state_ref, p_ref,
                        first_block):
    tb = t_ref.shape[0]
    n_chunks = tb // CHUNK

    def plane(idx, rs=slice(None)):
        return t_ref[rs, idx * LANES:(idx + 1) * LANES]

    lbl = lbl_ref[...]
    ex = jnp.exp(lbl - jnp.max(lbl, axis=0, keepdims=True))
    lb = ex[0:1, :] / jnp.sum(ex, axis=0, keepdims=True)
    tri = (lax.broadcasted_iota(jnp.int32, (CHUNK, CHUNK), 1)
           <= lax.broadcasted_iota(jnp.int32, (CHUNK, CHUNK), 0)).astype(_bf16)
    hw = hw_ref[...]

    live = {"state": jnp.where(first_block, 0.0, state_ref[...])}

    def rows_of(c):
        return pl.ds(c * CHUNK, CHUNK)

    def gates(c):
        if c % GATE_CHUNKS == 0:
            cs = range(c, min(c + GATE_CHUNKS, n_chunks))
            for ci, res in zip(cs, _chunk_gates([plane(1, rows_of(ci)) for ci in cs], lb, tri)):
                live[ci] = res

    def scores(c):
        k, b2 = live[c]
        live[c] = _chunk_scores(plane(0, rows_of(c)), k, b2)

    def output(c):
        rs = rows_of(c)
        o, live["state"] = _chunk_output(*live.pop(c), plane(2, rs), live["state"])
        ms = jnp.mean(o * o, axis=-1, keepdims=True)
        gate = plane(3, rs)
        ya = (o * lax.rsqrt(ms + EPS) * hw) * _silu(gate)
        ya_ref[rs, :] = ya.astype(ya_ref.dtype)
        if c == n_chunks - 1:
            state_ref[...] = live["state"]

    groups = []
    for g in range(n_chunks + 2):
        stages = [(output, g - 2), (scores, g - 1), (gates, g)]
        groups.append([functools.partial(fn, c) for fn, c in stages if 0 <= c < n_chunks])

    def conv(c):
        if c == 0:
            p_ref[0:SUBLANES, :] = jnp.where(first_block, 0.0, p_ref[tb:tb + SUBLANES, :])
        rs = rows_of(c)
        p_ref[pl.ds(SUBLANES + c * CHUNK, CHUNK), :] = plane(5, rs) * plane(6, rs)
        k_taps = cw_ref.shape[0]
        acc = None
        for j in range(k_taps):
            off = SUBLANES - (k_taps - 1) + j + c * CHUNK
            term = cw_ref[j:j + 1, :] * p_ref[off:off + CHUNK, :]
            acc = term if acc is None else acc + term
        yb = (plane(4, rs) * acc) * _silu(plane(7, rs))
        yb_ref[rs, :] = yb.astype(yb_ref.dtype)

    return groups, [functools.partial(conv, c) for c in range(n_chunks)]


def _stage_weight_tile(copies_of, tile, n_tiles, stage_ref, wbf_ref):
    for c in copies_of(tile):
        c.wait()

    def cast_rows(r, carry):
        rows = pl.ds(pl.multiple_of(r * CAST_ROWS, CAST_ROWS), CAST_ROWS)
        wbf_ref[rows, :] = stage_ref[rows, :].astype(wbf_ref.dtype)
        return carry

    lax.fori_loop(0, stage_ref.shape[0] // CAST_ROWS, cast_rows, 0)

    @pl.when(tile + 1 < n_tiles)
    def _():
        for c in copies_of(tile + 1):
            c.start()


def _head_weight_copies(w_hbm, stage_ref, sem, head, col_offsets):
    return [pltpu.make_async_copy(
        w_hbm.at[:, pl.ds(pl.multiple_of(off + head * LANES, LANES), LANES)],
        stage_ref.at[:, pl.ds(idx * LANES, LANES)], sem.at[idx])
        for idx, off in enumerate(col_offsets)]


def _proj_mixers_kernel(u_ref, w_hbm, lbl_ref, hw_ref, cw_ref, ya_ref, yb_ref,
                        stage_ref, wbf_ref, t0_ref, t1_ref, state_ref, p_ref, sem,
                        *, n_heads, n_i, col_offsets):
    s = pl.program_id(0)
    n_steps = n_heads * n_i
    head = jnp.minimum(s // n_i, n_heads - 1)
    copies = functools.partial(_head_weight_copies, w_hbm, stage_ref, sem,
                               col_offsets=col_offsets)

    @pl.when(s == 0)
    def _():
        for c in copies(0):
            c.start()
        t1_ref[...] = jnp.zeros_like(t1_ref)
        state_ref[...] = jnp.zeros_like(state_ref)
        p_ref[...] = jnp.zeros_like(p_ref)

    @pl.when((s % n_i == 0) & (s < n_steps))
    def _():
        _stage_weight_tile(copies, head, n_heads, stage_ref, wbf_ref)

    first_block = (jnp.maximum(s - 1, 0) % n_i) == 0
    bm = u_ref.shape[0]
    pieces = [(pl.ds(r, PIECE_ROWS), pl.ds(c, PIECE_COLS))
              for c in range(0, wbf_ref.shape[1], PIECE_COLS) for r in range(0, bm, PIECE_ROWS)]

    def step(t_write, t_read):
        groups, convs = _mixer_stage_groups(t_read, lbl_ref, hw_ref, cw_ref, ya_ref, yb_ref,
                                            state_ref, p_ref, first_block)
        for g, stages in enumerate(groups):
            for stage in stages:
                stage()
            if g < len(convs):
                convs[g]()
            lo, hi = (g * len(pieces)) // len(groups), ((g + 1) * len(pieces)) // len(groups)
            for rows, cols in pieces[lo:hi]:
                t_write[rows, cols] = _dot(u_ref[rows, :], wbf_ref[:, cols])

    @pl.when(s % 2 == 0)
    def _():
        step(t0_ref, t1_ref)

    @pl.when(s % 2 == 1)
    def _():
        step(t1_ref, t0_ref)


def _proj_mixers(u, w, lb_logits, hgrn_w_row, conv_taps, *, n_heads, col_offsets, bm):
    t_len, d = u.shape
    n_i = t_len // bm
    n_steps = n_heads * n_i
    n_layers_p1 = lb_logits.shape[0]
    k_taps = conv_taps.shape[0]
    width = n_heads * LANES
    tile_cols = N_PLANES * LANES

    def prev(s):
        return jnp.maximum(s - 1, 0)

    out_spec = pl.BlockSpec((bm, LANES), lambda s: (prev(s) % n_i, prev(s) // n_i))
    return pl.pallas_call(
        functools.partial(_proj_mixers_kernel, n_heads=n_heads, n_i=n_i,
                          col_offsets=tuple(col_offsets)),
        grid=(n_steps + 1,),
        in_specs=[
            pl.BlockSpec((bm, d), lambda s: (jnp.minimum(s, n_steps - 1) % n_i, 0)),
            pl.BlockSpec(memory_space=pl.ANY),
            pl.BlockSpec((n_layers_p1, LANES), lambda s: (0, prev(s) // n_i)),
            pl.BlockSpec((1, LANES), lambda s: (0, 0)),
            pl.BlockSpec((k_taps, LANES), lambda s: (0, prev(s) // n_i)),
        ],
        out_specs=[out_spec, out_spec],
        out_shape=[jax.ShapeDtypeStruct((t_len, width), _bf16)] * 2,
        scratch_shapes=[
            pltpu.VMEM((d, tile_cols), _f32),
            pltpu.VMEM((d, tile_cols), _bf16),
            pltpu.VMEM((bm, tile_cols), _f32),
            pltpu.VMEM((bm, tile_cols), _f32),
            pltpu.VMEM((LANES, LANES), _f32),
            pltpu.VMEM((bm + SUBLANES, LANES), _f32),
            pltpu.SemaphoreType.DMA((N_PLANES,)),
        ],
        compiler_params=pltpu.CompilerParams(
            dimension_semantics=("arbitrary",), vmem_limit_bytes=VMEM_LIMIT_BYTES),
        name="proj_mixers",
    )(u, w, lb_logits, hgrn_w_row, conv_taps)


def _column_pieces(n_cols):
    width = min(COLUMN_PIECE, n_cols)
    return [pl.ds(c, width) for c in range(0, n_cols, width)]


def _in_proj_kernel(u_ref, w_hbm, wo_ref, o_ref, wo_bf_ref, stage_ref, wbf_ref, sem,
                    *, n_tiles, n_i, col0):
    s = pl.program_id(0)
    wo_bf_ref[...] = wo_ref[...].astype(wo_bf_ref.dtype)
    bn = wbf_ref.shape[1]

    def copies(tile):
        cols = pl.ds(pl.multiple_of(col0 + tile * bn, LANES), bn)
        return [pltpu.make_async_copy(w_hbm.at[:, cols], stage_ref, sem.at[0])]

    @pl.when(s == 0)
    def _():
        for c in copies(0):
            c.start()

    @pl.when(s % n_i == 0)
    def _():
        _stage_weight_tile(copies, s // n_i, n_tiles, stage_ref, wbf_ref)

    for cols in _column_pieces(bn):
        o_ref[:, cols] = _dot(u_ref[...], wbf_ref[:, cols]).astype(o_ref.dtype)


def _in_proj(u, w, wo, *, col0, n_cols, bm, bn):
    t_len, d = u.shape
    n_i, n_tiles = t_len // bm, n_cols // bn
    n_steps = n_tiles * n_i
    wo_rows = wo.shape[0] // n_steps
    assert wo_rows * n_steps == wo.shape[0] and wo_rows % 16 == 0
    wo_spec = pl.BlockSpec((wo_rows, wo.shape[1]), lambda s: (s, 0))
    return pl.pallas_call(
        functools.partial(_in_proj_kernel, n_tiles=n_tiles, n_i=n_i, col0=col0),
        grid=(n_steps,),
        in_specs=[pl.BlockSpec((bm, d), lambda s: (s % n_i, 0)),
                  pl.BlockSpec(memory_space=pl.ANY),
                  wo_spec],
        out_specs=[pl.BlockSpec((bm, bn), lambda s: (s % n_i, s // n_i)), wo_spec],
        out_shape=[jax.ShapeDtypeStruct((t_len, n_cols), _bf16),
                   jax.ShapeDtypeStruct(wo.shape, _bf16)],
        scratch_shapes=[pltpu.VMEM((d, bn), _f32), pltpu.VMEM((d, bn), _bf16),
                        pltpu.SemaphoreType.DMA((1,))],
        compiler_params=pltpu.CompilerParams(
            dimension_semantics=("arbitrary",), vmem_limit_bytes=VMEM_LIMIT_BYTES),
        name="in_proj",
    )(u, w, wo)


def _merge_kernel(ya_ref, yb_ref, ga_ref, gb_ref, bias_ref, wa_hbm, wb_hbm, o_ref,
                  stage_ref, wbf_ref, sem, *, n_tiles, n_i):
    s = pl.program_id(0)
    width = wa_hbm.shape[0]
    bn = o_ref.shape[1]

    def copies(tile):
        cols = pl.ds(pl.multiple_of(tile * bn, LANES), bn)
        return [pltpu.make_async_copy(wa_hbm.at[:, cols], stage_ref.at[0:width, :], sem.at[0]),
                pltpu.make_async_copy(wb_hbm.at[:, cols], stage_ref.at[width:2 * width, :],
                                      sem.at[1])]

    @pl.when(s == 0)
    def _():
        for c in copies(0):
            c.start()

    @pl.when(s % n_i == 0)
    def _():
        _stage_weight_tile(copies, s // n_i, n_tiles, stage_ref, wbf_ref)

    for cols in _column_pieces(bn):
        for r in range(0, o_ref.shape[0], MERGE_PIECE_ROWS):
            rows = pl.ds(r, MERGE_PIECE_ROWS)
            za = _dot(ya_ref[rows, :], wbf_ref[0:width, cols])
            zb = _dot(yb_ref[rows, :], wbf_ref[width:2 * width, cols])
            gate_a = _sigmoid(ga_ref[rows, cols].astype(_f32) + bias_ref[0:1, cols])
            gate_b = _sigmoid(gb_ref[rows, cols].astype(_f32) + bias_ref[1:2, cols])
            o_ref[rows, cols] = (gate_a * za + gate_b * zb).astype(o_ref.dtype)


def _merge(ya, yb, gates, gate_bias, wa, wb, *, ga_off, gb_off, bm, bn):
    t_len, width = ya.shape
    d = wa.shape[1]
    ga_base, gb_base = ga_off // bn, gb_off // bn
    n_i, n_tiles = t_len // bm, d // bn
    return pl.pallas_call(
        functools.partial(_merge_kernel, n_tiles=n_tiles, n_i=n_i),
        grid=(n_tiles * n_i,),
        in_specs=[
            pl.BlockSpec((bm, width), lambda s: (s % n_i, 0)),
            pl.BlockSpec((bm, width), lambda s: (s % n_i, 0)),
            pl.BlockSpec((bm, bn), lambda s: (s % n_i, ga_base + s // n_i)),
            pl.BlockSpec((bm, bn), lambda s: (s % n_i, gb_base + s // n_i)),
            pl.BlockSpec((2, bn), lambda s: (0, s // n_i)),
            pl.BlockSpec(memory_space=pl.ANY),
            pl.BlockSpec(memory_space=pl.ANY),
        ],
        out_specs=pl.BlockSpec((bm, bn), lambda s: (s % n_i, s // n_i)),
        out_shape=jax.ShapeDtypeStruct((t_len, d), _bf16),
        scratch_shapes=[pltpu.VMEM((2 * width, bn), _f32), pltpu.VMEM((2 * width, bn), _bf16),
                        pltpu.SemaphoreType.DMA((2,))],
        compiler_params=pltpu.CompilerParams(
            dimension_semantics=("arbitrary",), vmem_limit_bytes=VMEM_LIMIT_BYTES),
        name="merge",
    )(ya, yb, gates, gates, gate_bias, wa, wb)


def _out_proj_kernel(m_ref, wo_ref, x_ref, fw_ref, o_ref):
    for cols in _column_pieces(o_ref.shape[1]):
        o_ref[:, cols] = x_ref[:, cols] + _dot(m_ref[...], wo_ref[:, cols])
    h = o_ref[...]
    ms = jnp.mean(h * h, axis=-1, keepdims=True)
    o_ref[...] = h * lax.rsqrt(ms + EPS) * fw_ref[...]


def _out_proj(merged, wo, x2d, fw_row, *, bm):
    t_len, d = x2d.shape
    return pl.pallas_call(
        _out_proj_kernel,
        grid=(t_len // bm,),
        in_specs=[
            pl.BlockSpec((bm, d), lambda i: (i, 0)),
            pl.BlockSpec((d, d), lambda i: (0, 0), pipeline_mode=pl.Buffered(1)),
            pl.BlockSpec((bm, d), lambda i: (i, 0)),
            pl.BlockSpec((1, d), lambda i: (0, 0)),
        ],
        out_specs=pl.BlockSpec((bm, d), lambda i: (i, 0)),
        out_shape=jax.ShapeDtypeStruct((t_len, d), _f32),
        compiler_params=pltpu.CompilerParams(
            dimension_semantics=("parallel",), vmem_limit_bytes=OUT_PROJ_VMEM_LIMIT_BYTES),
        name="out_proj",
    )(merged, wo, x2d, fw_row)


def _block_sizes(t_len, d):
    return dict(prenorm_rows=min(512, t_len), mixers_rows=min(1024, t_len),
                gates_rows=min(1024, t_len), gates_cols=min(1024, d),
                merge_rows=min(1024, t_len), merge_cols=min(1024, d), out_rows=min(256, t_len))


def kernel(x, norm_w, w_in, lb_logits, hgrn_norm_w, conv_w, w_branch_a, w_branch_b,
           gate_bias, w_out, final_norm_w):
    bsz, t_len, d = x.shape
    assert bsz == 1 and norm_w.shape[0] == 1, "single sequence, single layer"
    key_w = lb_logits.shape[1]
    val_w = w_branch_a.shape[1]
    conv_wd = conv_w.shape[2]
    assert hgrn_norm_w.shape[1] == LANES and key_w == val_w == conv_wd
    n_heads = key_w // LANES
    sizes = (key_w, key_w, val_w, val_w, conv_wd, conv_wd, conv_wd, conv_wd, d, d)
    offs = [sum(sizes[:i]) for i in range(len(sizes))]
    assert w_in.shape[2] == sum(sizes)

    x2d = x.reshape(t_len, d)
    w2d = w_in.reshape(d, w_in.shape[2])
    blk = _block_sizes(t_len, d)
    u = _prenorm(x2d, norm_w, block_rows=blk["prenorm_rows"])
    ya, yb = _proj_mixers(u, w2d, lb_logits, hgrn_norm_w, conv_w[0], n_heads=n_heads,
                          col_offsets=offs[:N_PLANES], bm=blk["mixers_rows"])
    gates, wo_bf = _in_proj(u, w2d, w_out[0], col0=offs[N_PLANES], n_cols=2 * d,
                            bm=blk["gates_rows"], bn=blk["gates_cols"])
    merged = _merge(ya, yb, gates, gate_bias[0], w_branch_a[0], w_branch_b[0],
                    ga_off=0, gb_off=d, bm=blk["merge_rows"], bn=blk["merge_cols"])
    out = _out_proj(merged, wo_bf, x2d, final_norm_w.reshape(1, d), bm=blk["out_rows"])
    return out.reshape(bsz, t_len, d)
```
